```python
import math
import jax, jax.numpy as jnp
from jax import lax
import numpy as np

D_MODEL = 1024
BATCH = 2
SEQ = 8192
DEPTH = 4
DEC_BATCH = 32
DEC_SEQ = 4
PAST_LEN = 8192
PAGE_SIZE = 128

N_MIXERS = 2
N_RET_LAYERS = (DEPTH + 1) // 2
N_DA_LAYERS = DEPTH // 2

RET_HEADS = 4
RET_DK = D_MODEL // RET_HEADS
RET_DV = 2 * RET_DK
RET_QK = RET_HEADS * RET_DK
RET_V = RET_HEADS * RET_DV
RET_CHUNK = 128
RET_THETA = 10000.0

DA_HEADS = 8
DA_DK = D_MODEL // (2 * DA_HEADS)
DA_DV = 2 * DA_DK
DA_QK = DA_HEADS * 2 * DA_DK
DA_V = DA_HEADS * DA_DV
DA_ROT = DA_DK // 4
ROPE_THETA = 500000.0
Q_BLOCK = 128

PEER_HEADS = 8
PEER_NKEYS = 128
PEER_N = PEER_NKEYS * PEER_NKEYS
PEER_DKEY = 256
PEER_TOPK = 16
PEER_BLOCK = 256

LN_EPS = 1e-5
NORM_EPS = 1e-5
ALPHA = (2 * DEPTH) ** 0.25
BETA = (8 * DEPTH) ** -0.25

kernel_name = "retnet_diffattn_peer_hybrid_step"

F32 = jnp.float32


def layer_norm(x, g, b):
    xf = x.astype(F32)
    mu = xf.mean(-1, keepdims=True)
    var = jnp.square(xf - mu).mean(-1, keepdims=True)
    y = (xf - mu) * lax.rsqrt(var + LN_EPS)
    return (y * g.astype(F32) + b.astype(F32)).astype(x.dtype)


def rotary(x, pos, rot_dim, theta):
    half = rot_dim // 2
    inv = 1.0 / (theta ** (jnp.arange(half, dtype=F32) / half))
    ang = pos.astype(F32)[:, None] * inv[None, :]
    ang = ang.reshape(ang.shape[:1] + (1,) * (x.ndim - 3) + (half,))
    cos, sin = jnp.cos(ang), jnp.sin(ang)
    xr = x[..., :rot_dim].astype(F32)
    x1, x2 = xr[..., :half], xr[..., half:]
    rot = jnp.concatenate([x1 * cos - x2 * sin, x2 * cos + x1 * sin], -1).astype(x.dtype)
    return jnp.concatenate([rot, x[..., rot_dim:]], -1)


def ret_log_gamma():
    return jnp.log(1.0 - 2.0 ** (-5.0 - jnp.arange(RET_HEADS, dtype=F32)))


def retention_chunk(q, k, v, r_prev, log_gamma):
    L = q.shape[1]
    idx = jnp.arange(L, dtype=F32)
    diff = idx[:, None] - idx[None, :]
    decay = jnp.where(diff >= 0, jnp.exp(log_gamma[:, None, None] * jnp.maximum(diff, 0.0)), 0.0)
    qf, kf, vf = q.astype(F32), k.astype(F32), v.astype(F32)
    inner = jnp.einsum('bihd,bjhd->bhij', qf, kf) * decay[None]
    out = jnp.einsum('bhij,bjhe->bihe', inner, vf)
    q_decay = jnp.exp(log_gamma[None, :] * (idx[:, None] + 1.0))
    out = out + jnp.einsum('bihd,bhde->bihe', qf, r_prev) * q_decay[None, :, :, None]
    k_decay = jnp.exp(log_gamma[None, :] * (L - 1.0 - idx[:, None]))
    r_new = jnp.exp(log_gamma * L)[None, :, None, None] * r_prev + jnp.einsum('bjhd,bjhe->bhde', kf * k_decay[None, :, :, None], vf)
    return out, r_new


def retention_mixer(x, pos, r0, w_in, w_out):
    B, S, _ = x.shape
    q, k, v, g = jnp.split(x @ w_in, [RET_QK, 2 * RET_QK, 2 * RET_QK + RET_V], axis=-1)
    q = rotary(q.reshape(B, S, RET_HEADS, RET_DK), pos, RET_DK, RET_THETA)
    k = rotary(k.reshape(B, S, RET_HEADS, RET_DK), pos, RET_DK, RET_THETA) * (RET_DK ** -0.5)
    v = v.reshape(B, S, RET_HEADS, RET_DV)
    chunk = RET_CHUNK if S % RET_CHUNK == 0 else S
    nc = S // chunk
    log_gamma = ret_log_gamma()

    def to_chunks(t):
        return jnp.moveaxis(t.reshape((B, nc, chunk) + t.shape[2:]), 1, 0)

    def step(r, qkv):
        qc, kc, vc = qkv
        o, r = retention_chunk(qc, kc, vc, r, log_gamma)
        return r, o

    r_final, o = lax.scan(step, r0.astype(F32), (to_chunks(q), to_chunks(k), to_chunks(v)))
    o = jnp.moveaxis(o, 0, 1).reshape(B, S, RET_HEADS, RET_DV)
    mu = o.mean(-1, keepdims=True)
    var = jnp.square(o - mu).mean(-1, keepdims=True)
    o = ((o - mu) * lax.rsqrt(var + NORM_EPS)).reshape(B, S, RET_V)
    y = (jax.nn.silu(g.astype(F32)) * o).astype(x.dtype) @ w_out
    return y, r_final


def diff_project(x, pos, w_in):
    B, S, _ = x.shape
    q, k, v = jnp.split(x @ w_in, [DA_QK, 2 * DA_QK], axis=-1)
    q = rotary(q.reshape(B, S, DA_HEADS, 2, DA_DK), pos, DA_ROT, ROPE_THETA)
    k = rotary(k.reshape(B, S, DA_HEADS, 2, DA_DK), pos, DA_ROT, ROPE_THETA)
    v = v.reshape(B, S, DA_HEADS, DA_DV)
    return q, k, v


def diff_lambda(lam_vec, layer_idx):
    lam_init = 0.8 - 0.6 * math.exp(-0.3 * layer_idx)
    lf = lam_vec.astype(F32)
    lam = jnp.exp(jnp.sum(lf[0] * lf[1])) - jnp.exp(jnp.sum(lf[2] * lf[3])) + lam_init
    return lam, lam_init


def diff_out(o, lam_init, subln_g, w_out, dtype):
    B, S = o.shape[:2]
    o = o * lax.rsqrt(jnp.mean(jnp.square(o), -1, keepdims=True) + NORM_EPS)
    o = o * subln_g.astype(F32) * (1.0 - lam_init)
    return o.reshape(B, S, DA_V).astype(dtype) @ w_out


def diff_attn_prompt(x, w_in, w_out, lam_vec, subln_g, layer_idx):
    B, S, _ = x.shape
    pos = jnp.arange(S)
    q, k, v = diff_project(x, pos, w_in)
    lam, lam_init = diff_lambda(lam_vec, layer_idx)
    nb = S // Q_BLOCK
    qb = jnp.moveaxis(q.reshape(B, nb, Q_BLOCK, DA_HEADS, 2, DA_DK), 1, 0)
    kf, vf = k.astype(F32), v.astype(F32)
    kpos = jnp.arange(S)
    scale = DA_DK ** -0.5

    def block(args):
        qi, b = args
        s = jnp.einsum('bqhcd,bkhcd->bhcqk', qi.astype(F32), kf) * scale
        qpos = b * Q_BLOCK + jnp.arange(Q_BLOCK)
        s = jnp.where((kpos[None, :] <= qpos[:, None])[None, None, None], s, -jnp.inf)
        p = jax.nn.softmax(s, axis=-1)
        a = p[:, :, 0] - lam * p[:, :, 1]
        return jnp.einsum('bhqk,bkhe->bqhe', a, vf)

    o = lax.map(block, (qb, jnp.arange(nb)))
    o = jnp.moveaxis(o, 0, 1).reshape(B, S, DA_HEADS, DA_DV)
    return diff_out(o, lam_init, subln_g, w_out, x.dtype), k, v


def diff_attn_sample(x, cache_k, cache_v, page_table, w_in, w_out, lam_vec, subln_g, layer_idx):
    B, T, _ = x.shape
    past = page_table.shape[1] * PAGE_SIZE
    pos = PAST_LEN + jnp.arange(T)
    q, k, v = diff_project(x, pos, w_in)
    lam, lam_init = diff_lambda(lam_vec, layer_idx)
    kp = cache_k[page_table].reshape(B, past, DA_HEADS, 2, DA_DK)
    vp = cache_v[page_table].reshape(B, past, DA_HEADS, DA_DV)
    scale = DA_DK ** -0.5
    qf = q.astype(F32)
    s_past = jnp.einsum('bqhcd,bkhcd->bhcqk', qf, kp.astype(F32)) * scale
    s_new = jnp.einsum('bqhcd,bkhcd->bhcqk', qf, k.astype(F32)) * scale
    causal = jnp.arange(T)[None, :] <= jnp.arange(T)[:, None]
    s_new = jnp.where(causal[None, None, None], s_new, -jnp.inf)
    p = jax.nn.softmax(jnp.concatenate([s_past, s_new], axis=-1), axis=-1)
    a = p[:, :, 0] - lam * p[:, :, 1]
    o = jnp.einsum('bhqk,bkhe->bqhe', a[..., :past], vp.astype(F32)) + jnp.einsum('bhqk,bkhe->bqhe', a[..., past:], v.astype(F32))
    return diff_out(o, lam_init, subln_g, w_out, x.dtype), k, v


def peer_ffn(x, w_q, sub_keys, u_tab, v_tab):
    B, S, D = x.shape
    T = B * S
    blk = min(PEER_BLOCK, T)
    pad = (-T) % blk
    xt = jnp.pad(x.reshape(T, D), ((0, pad), (0, 0)))
    nb = (T + pad) // blk
    skf = sub_keys.astype(F32)

    def block(xb):
        q = (xb @ w_q).reshape(blk, PEER_HEADS, 2, PEER_DKEY // 2).astype(F32)
        s = jnp.einsum('thcd,cnd->thcn', q, skf)
        s1, i1 = lax.top_k(s[:, :, 0], PEER_TOPK)
        s2, i2 = lax.top_k(s[:, :, 1], PEER_TOPK)
        cand = (s1[..., :, None] + s2[..., None, :]).reshape(blk, PEER_HEADS, PEER_TOPK * PEER_TOPK)
        sc, ci = lax.top_k(cand, PEER_TOPK)
        e1 = jnp.take_along_axis(i1, ci // PEER_TOPK, axis=-1)
        e2 = jnp.take_along_axis(i2, ci % PEER_TOPK, axis=-1)
        eid = e1 * PEER_NKEYS + e2
        g = jax.nn.softmax(sc, axis=-1)
        h = jax.nn.gelu(jnp.einsum('thkd,td->thk', u_tab[eid].astype(F32), xb.astype(F32)), approximate=False)
        return jnp.einsum('thk,thkd->td', g * h, v_tab[eid].astype(F32)).astype(x.dtype)

    y = lax.map(block, xt.reshape(nb, blk, D))
    return y.reshape(nb * blk, D)[:T].reshape(B, S, D)


def setup_inputs(seed: int = 0) -> dict:
    key = jax.random.key(seed)
    ks = iter(jax.random.split(key, 32))

    def nrm(shape, scale):
        return jax.random.normal(next(ks), shape, F32) * scale

    n_pages = PAST_LEN // PAGE_SIZE
    n_used = DEC_BATCH * n_pages
    n_phys = n_used + max(1, n_used // 4)
    d_inv = D_MODEL ** -0.5

    x_prompt = nrm((BATCH, SEQ, D_MODEL), 1.0)
    x_sample = nrm((DEC_BATCH, DEC_SEQ, D_MODEL), 1.0)
    state_ret_l0 = nrm((DEC_BATCH, RET_HEADS, RET_DK, RET_DV), 0.05)
    cache_k_l1 = nrm((n_phys, PAGE_SIZE, DA_HEADS, 2, DA_DK), 1.0)
    cache_v_l1 = nrm((n_phys, PAGE_SIZE, DA_HEADS, DA_DV), 1.0)
    state_ret_l2 = nrm((DEC_BATCH, RET_HEADS, RET_DK, RET_DV), 0.05)
    cache_k_l3 = nrm((n_phys, PAGE_SIZE, DA_HEADS, 2, DA_DK), 1.0)
    cache_v_l3 = nrm((n_phys, PAGE_SIZE, DA_HEADS, DA_DV), 1.0)
    page_table = jax.random.permutation(next(ks), n_phys)[:n_used].reshape(DEC_BATCH, n_pages).astype(jnp.int32)

    ret_cols = jnp.concatenate([jnp.ones((2 * RET_QK,), F32), jnp.full((RET_V,), BETA, F32), jnp.ones((RET_V,), F32)])
    ret_w_in = nrm((N_RET_LAYERS, D_MODEL, 2 * RET_QK + 2 * RET_V), d_inv) * ret_cols
    ret_w_out = nrm((N_RET_LAYERS, RET_V, D_MODEL), BETA * RET_V ** -0.5)
    da_cols = jnp.concatenate([jnp.ones((2 * DA_QK,), F32), jnp.full((DA_V,), BETA, F32)])
    da_w_in = nrm((N_DA_LAYERS, D_MODEL, 2 * DA_QK + DA_V), d_inv) * da_cols
    da_w_out = nrm((N_DA_LAYERS, DA_V, D_MODEL), BETA * DA_V ** -0.5)
    da_lambda = nrm((N_DA_LAYERS, 4, DA_DK), 0.1)
    da_subln_g = 1.0 + nrm((N_DA_LAYERS, DA_DV), 0.01)
    peer_w_q = nrm((DEPTH, D_MODEL, PEER_HEADS * PEER_DKEY), d_inv)
    peer_sub_keys = nrm((DEPTH, 2, PEER_NKEYS, PEER_DKEY // 2), (PEER_DKEY // 2) ** -0.5)
    peer_u = nrm((DEPTH, PEER_N, D_MODEL), d_inv)
    peer_v = nrm((DEPTH, PEER_N, D_MODEL), BETA * PEER_HEADS ** -0.5)
    ln_g = 1.0 + nrm((DEPTH, 2, D_MODEL), 0.01)
    ln_b = nrm((DEPTH, 2, D_MODEL), 0.01)
    return {"x_prompt": x_prompt, "x_sample": x_sample,
            "state_ret_l0": state_ret_l0, "cache_k_l1": cache_k_l1, "cache_v_l1": cache_v_l1,
            "state_ret_l2": state_ret_l2, "cache_k_l3": cache_k_l3, "cache_v_l3": cache_v_l3,
            "page_table": page_table,
            "ret_w_in": ret_w_in, "ret_w_out": ret_w_out,
            "da_w_in": da_w_in, "da_w_out": da_w_out, "da_lambda": da_lambda, "da_subln_g": da_subln_g,
            "peer_w_q": peer_w_q, "peer_sub_keys": peer_sub_keys, "peer_u": peer_u, "peer_v": peer_v,
            "ln_g": ln_g, "ln_b": ln_b}


def reference(x_prompt, x_sample, state_ret_l0, cache_k_l1, cache_v_l1, state_ret_l2, cache_k_l3, cache_v_l3,
              page_table, ret_w_in, ret_w_out, da_w_in, da_w_out, da_lambda, da_subln_g,
              peer_w_q, peer_sub_keys, peer_u, peer_v, ln_g, ln_b):
    ret_states = [state_ret_l0, state_ret_l2]
    da_caches = [(cache_k_l1, cache_v_l1), (cache_k_l3, cache_v_l3)]
    xp, xs = x_prompt, x_sample
    bp, sp = xp.shape[:2]
    ts = xs.shape[1]
    pos_p = jnp.arange(sp)
    pos_s = PAST_LEN + jnp.arange(ts)
    new_states = []
    for i in range(DEPTH):
        j = i // N_MIXERS
        if i % N_MIXERS == 0:
            r0 = jnp.zeros((bp, RET_HEADS, RET_DK, RET_DV), F32)
            mp, rp = retention_mixer(xp, pos_p, r0, ret_w_in[j], ret_w_out[j])
            ms, rs = retention_mixer(xs, pos_s, ret_states[j], ret_w_in[j], ret_w_out[j])
            new_states.append((rp, rs))
        else:
            ck, cv = da_caches[j]
            mp, kp, vp = diff_attn_prompt(xp, da_w_in[j], da_w_out[j], da_lambda[j], da_subln_g[j], i)
            ms, ksm, vsm = diff_attn_sample(xs, ck, cv, page_table, da_w_in[j], da_w_out[j], da_lambda[j], da_subln_g[j], i)
            new_states.append((kp, vp, ksm, vsm))
        xp = layer_norm(ALPHA * xp + mp, ln_g[i, 0], ln_b[i, 0])
        xs = layer_norm(ALPHA * xs + ms, ln_g[i, 0], ln_b[i, 0])
        wq, skeys, ut, vt = peer_w_q[i], peer_sub_keys[i], peer_u[i], peer_v[i]
        xp = layer_norm(ALPHA * xp + peer_ffn(xp, wq, skeys, ut, vt), ln_g[i, 1], ln_b[i, 1])
        xs = layer_norm(ALPHA * xs + peer_ffn(xs, wq, skeys, ut, vt), ln_g[i, 1], ln_b[i, 1])
    (rp0, rs0), (kp1, vp1, ks1, vs1), (rp2, rs2), (kp3, vp3, ks3, vs3) = new_states
    return (xp, xs, rp0, rs0, kp1, vp1, ks1, vs1, rp2, rs2, kp3, vp3, ks3, vs3)
```

```python
import functools
import math

import jax
import jax.numpy as jnp
from jax import lax
from jax.experimental import pallas as pl
from jax.experimental.pallas import tpu as pltpu

F32 = jnp.float32
BF16 = jnp.bfloat16

D_MODEL = 1024
DEPTH = 4
PAST_LEN = 8192
PAGE_SIZE = 128

RET_HEADS = 4
RET_DK = 256
RET_DV = 512
RET_QK = RET_HEADS * RET_DK
RET_V = RET_HEADS * RET_DV
RET_CHUNK = 128
RET_THETA = 10000.0

DA_HEADS = 8
DA_DK = 64
DA_DV = 128
DA_QK = DA_HEADS * 2 * DA_DK
DA_V = DA_HEADS * DA_DV
DA_ROT = DA_DK // 4
ROPE_THETA = 500000.0

PEER_HEADS = 8
PEER_NKEYS = 128
PEER_N = PEER_NKEYS * PEER_NKEYS
PEER_DKEY = 256
PEER_TOPK = 16

LN_EPS = 1e-5
NORM_EPS = 1e-5
ALPHA = (2 * DEPTH) ** 0.25

LANES = 128
VMEM_LIMIT = 56 * 1024 * 1024

NT_DIMS = (((1,), (1,)), ((), ()))
TN_DIMS = (((0,), (0,)), ((), ()))


def _cparams(sem):
    return pltpu.CompilerParams(dimension_semantics=sem, vmem_limit_bytes=VMEM_LIMIT)


def _layer_norm(z, g, b):
    mu = jnp.mean(z, axis=-1, keepdims=True)
    d = z - mu
    var = jnp.mean(d * d, axis=-1, keepdims=True)
    return d * lax.rsqrt(var + LN_EPS) * g + b


def _proj_kernel(*refs, rot, n_tab, scales, n_scale_split):
    x_ref, w_ref = refs[0], refs[1]
    tabs = refs[2:2 + n_tab]
    outs = refs[2 + n_tab:]
    acc = jnp.dot(x_ref[...].astype(BF16), w_ref[...], preferred_element_type=F32)
    tn = acc.shape[1]
    if rot == "full":
        cos, sin = tabs[0][...], tabs[1][...]
        x1, x2 = acc[:, :LANES], acc[:, LANES:]
        acc = jnp.concatenate([x1 * cos - x2 * sin, x2 * cos + x1 * sin], axis=1)
    elif rot == "partial":
        c, s_lo, s_hi = tabs[0][...], tabs[1][...], tabs[2][...]
        blocks = []
        for cb in range(tn // LANES):
            blk = acc[:, cb * LANES:(cb + 1) * LANES]
            nxt = pltpu.roll(blk, LANES - DA_ROT // 2, 1)
            prv = pltpu.roll(blk, DA_ROT // 2, 1)
            blocks.append(blk * c + nxt * s_lo + prv * s_hi)
        acc = jnp.concatenate(blocks, axis=1) if len(blocks) > 1 else blocks[0]
    if scales is not None:
        j = pl.program_id(1)
        acc = acc * jnp.where(j < n_scale_split, F32(scales[0]), F32(scales[1]))
    for o in outs:
        o[...] = acc.astype(o.dtype)


def _proj(x, w, *, tm, tn, out_dtypes, rot=None, tabs=(), scales=None, n_scale_split=0):
    t, k = x.shape
    n = w.shape[1]
    tm = min(tm, t)
    grid = (t // tm, n // tn)
    in_specs = [pl.BlockSpec((tm, k), lambda i, j: (i, 0)),
                pl.BlockSpec((k, tn), lambda i, j: (0, j))]
    for tab in tabs:
        nrb = tab.shape[0] // tm
        in_specs.append(pl.BlockSpec((tm, LANES), lambda i, j, nrb=nrb: (i % nrb, 0)))
    out_shape = [jax.ShapeDtypeStruct((t, n), dt) for dt in out_dtypes]
    out_specs = [pl.BlockSpec((tm, tn), lambda i, j: (i, j)) for _ in out_dtypes]
    kern = functools.partial(_proj_kernel, rot=rot, n_tab=len(tabs), scales=scales,
                             n_scale_split=n_scale_split)
    return pl.pallas_call(kern, grid=grid, in_specs=in_specs, out_specs=out_specs,
                          out_shape=out_shape,
                          compiler_params=_cparams(("parallel", "arbitrary")),
                          name="proj")(x, w, *tabs)


def _out_ln_kernel(y_ref, w_ref, x_ref, g_ref, b_ref, o_ref):
    m = jnp.dot(y_ref[...].astype(BF16), w_ref[...], preferred_element_type=F32)
    o_ref[...] = _layer_norm(ALPHA * x_ref[...] + m, g_ref[...], b_ref[...])


def _out_ln(y, w, x, g, b, *, tm=512):
    t, k = y.shape
    n = w.shape[1]
    tm = min(tm, t)
    return pl.pallas_call(
        _out_ln_kernel, grid=(t // tm,),
        in_specs=[pl.BlockSpec((tm, k), lambda i: (i, 0)),
                  pl.BlockSpec((k, n), lambda i: (0, 0)),
                  pl.BlockSpec((tm, n), lambda i: (i, 0)),
                  pl.BlockSpec((1, n), lambda i: (0, 0)),
                  pl.BlockSpec((1, n), lambda i: (0, 0))],
        out_specs=pl.BlockSpec((tm, n), lambda i: (i, 0)),
        out_shape=jax.ShapeDtypeStruct((t, n), F32),
        compiler_params=_cparams(("parallel",)),
        name="out_ln")(y, w, x, g.reshape(1, n), b.reshape(1, n))


def _ret_kernel(q_ref, k_ref, v_ref, g_ref, r0_ref, dm_ref, qd_ref, kd_ref, gl_ref,
                y_ref, rout_ref, r_scr, *, nc):
    c = pl.program_id(2)

    @pl.when(c == 0)
    def _():
        r_scr[...] = r0_ref[0, 0]

    q = q_ref[0]
    k = k_ref[0]
    vb = v_ref[0].astype(BF16)
    qb = q.astype(BF16)
    qk = lax.dot_general(qb, k.astype(BF16), NT_DIMS, preferred_element_type=F32)
    inner = (qk * dm_ref[0]).astype(BF16)
    r = r_scr[...]
    o = jnp.dot(inner, vb, preferred_element_type=F32)
    o = o + jnp.dot(qb, r.astype(BF16), preferred_element_type=F32) * qd_ref[0]
    kdec = (k * kd_ref[0]).astype(BF16)
    r_new = gl_ref[0] * r + lax.dot_general(kdec, vb, TN_DIMS, preferred_element_type=F32)
    r_scr[...] = r_new

    mu = jnp.mean(o, axis=-1, keepdims=True)
    d = o - mu
    var = jnp.mean(d * d, axis=-1, keepdims=True)
    on = d * lax.rsqrt(var + NORM_EPS)
    gg = g_ref[0]
    y_ref[0] = (gg * jax.nn.sigmoid(gg) * on).astype(y_ref.dtype)

    @pl.when(c == nc - 1)
    def _():
        rout_ref[0, 0] = r_new


def _ret_decay_tables(chunk, true_len):
    lg = jnp.log(1.0 - 2.0 ** (-5.0 - jnp.arange(RET_HEADS, dtype=F32)))
    idx = jnp.arange(chunk, dtype=F32)
    diff = idx[:, None] - idx[None, :]
    dm = jnp.where(diff >= 0, jnp.exp(lg[:, None, None] * jnp.maximum(diff, 0.0)), 0.0)
    qd = jnp.exp(lg[:, None] * (idx[None, :] + 1.0))[:, :, None]
    kd = jnp.where(idx[None, :] < true_len,
                   jnp.exp(lg[:, None] * (true_len - 1.0 - idx[None, :])), 0.0)[:, :, None]
    gl = jnp.exp(lg * true_len)[:, None, None]
    return dm.astype(F32), qd.astype(F32), kd.astype(F32), gl.astype(F32)


def _retention(qk, vg, r0, *, batch, seq, chunk, true_len):
    nc = seq // chunk
    qk3 = qk.reshape(batch, seq, 2 * RET_QK)
    vg3 = vg.reshape(batch, seq, 2 * RET_V)
    dm, qd, kd, gl = _ret_decay_tables(chunk, true_len)
    kern = functools.partial(_ret_kernel, nc=nc)
    y, r_fin = pl.pallas_call(
        kern, grid=(batch, RET_HEADS, nc),
        in_specs=[
            pl.BlockSpec((1, chunk, RET_DK), lambda b, h, c: (b, c, h)),
            pl.BlockSpec((1, chunk, RET_DK), lambda b, h, c: (b, c, RET_HEADS + h)),
            pl.BlockSpec((1, chunk, RET_DV), lambda b, h, c: (b, c, h)),
            pl.BlockSpec((1, chunk, RET_DV), lambda b, h, c: (b, c, RET_HEADS + h)),
            pl.BlockSpec((1, 1, RET_DK, RET_DV), lambda b, h, c: (b, h, 0, 0)),
            pl.BlockSpec((1, chunk, chunk), lambda b, h, c: (h, 0, 0)),
            pl.BlockSpec((1, chunk, 1), lambda b, h, c: (h, 0, 0)),
            pl.BlockSpec((1, chunk, 1), lambda b, h, c: (h, 0, 0)),
            pl.BlockSpec((1, 1, 1), lambda b, h, c: (h, 0, 0)),
        ],
        out_specs=[
            pl.BlockSpec((1, chunk, RET_DV), lambda b, h, c: (b, c, h)),
            pl.BlockSpec((1, 1, RET_DK, RET_DV), lambda b, h, c: (b, h, 0, 0)),
        ],
        out_shape=[jax.ShapeDtypeStruct((batch, seq, RET_V), BF16),
                   jax.ShapeDtypeStruct((batch, RET_HEADS, RET_DK, RET_DV), F32)],
        scratch_shapes=[pltpu.VMEM((RET_DK, RET_DV), F32)],
        compiler_params=_cparams(("parallel", "parallel", "arbitrary")),
        name="retention")(qk3, qk3, vg3, vg3, r0, dm, qd, kd, gl)
    return y.reshape(batch * seq, RET_V), r_fin


def _diff_lambda(lam_ref, layer_idx):
    lv = lam_ref[...]
    lam_init = 0.8 - 0.6 * math.exp(-0.3 * layer_idx)
    t1 = jnp.sum(lv[0:1, :] * lv[1:2, :], axis=-1, keepdims=True)
    t2 = jnp.sum(lv[2:3, :] * lv[3:4, :], axis=-1, keepdims=True)
    return jnp.exp(t1) - jnp.exp(t2) + lam_init, lam_init


def _attn_kernel(qi_ref, kj_ref, q_ref, k_ref, v_ref, lam_ref, sg_ref, o_ref,
                 qz_scr, m_scr, l_scr, acc_scr, *, tq, tk, layer_idx):
    p = pl.program_id(2)
    qi = qi_ref[p]
    kj = kj_ref[p]
    last_kj = (qi * tq + tq - 1) // tk

    @pl.when(kj == 0)
    def _():
        q = q_ref[0]
        lane = lax.broadcasted_iota(jnp.int32, q.shape, 1)
        zero = jnp.zeros_like(q)
        qz_scr[0:tq, :] = jnp.where(lane < DA_DK, q, zero)
        qz_scr[tq:2 * tq, :] = jnp.where(lane >= DA_DK, q, zero)
        m_scr[...] = jnp.full(m_scr.shape, -jnp.inf, F32)
        l_scr[...] = jnp.zeros(l_scr.shape, F32)
        acc_scr[...] = jnp.zeros(acc_scr.shape, F32)

    s = lax.dot_general(qz_scr[...], k_ref[0], NT_DIMS, preferred_element_type=F32)
    row = lax.broadcasted_iota(jnp.int32, s.shape, 0)
    col = lax.broadcasted_iota(jnp.int32, s.shape, 1)
    qpos = qi * tq + jnp.where(row >= tq, row - tq, row)
    kpos = kj * tk + col
    s = jnp.where(kpos <= qpos, s, -jnp.inf)
    m_old = m_scr[...][:, 0:1]
    m_new = jnp.maximum(m_old, jnp.max(s, axis=-1, keepdims=True))
    alpha = jnp.exp(m_old - m_new)
    pm = jnp.exp(s - m_new)
    l_new = alpha * l_scr[...][:, 0:1] + jnp.sum(pm, axis=-1, keepdims=True)
    acc_scr[...] = alpha * acc_scr[...] + jnp.dot(pm.astype(BF16), v_ref[0],
                                                  preferred_element_type=F32)
    m_scr[...] = jnp.broadcast_to(m_new, m_scr.shape)
    l_scr[...] = jnp.broadcast_to(l_new, l_scr.shape)

    @pl.when(kj == last_kj)
    def _():
        lam, lam_init = _diff_lambda(lam_ref, layer_idx)
        o = acc_scr[...] / l_scr[...]
        od = o[0:tq, :] - lam * o[tq:2 * tq, :]
        od = od * lax.rsqrt(jnp.mean(od * od, axis=-1, keepdims=True) + NORM_EPS)
        o_ref[0] = (od * sg_ref[...] * (1.0 - lam_init)).astype(o_ref.dtype)


def _attn_prompt(qb, kb, vb, lam_vec, subln_g, *, batch, seq, layer_idx, tq=512, tk=1024):
    tq = min(tq, seq)
    tk = min(tk, seq)
    pairs = [(i, j) for i in range(seq // tq) for j in range((i * tq + tq - 1) // tk + 1)]
    qi = jnp.asarray([pq for pq, _ in pairs], jnp.int32)
    kj = jnp.asarray([pk for _, pk in pairs], jnp.int32)
    q3 = qb.reshape(batch, seq, DA_QK)
    k3 = kb.reshape(batch, seq, DA_QK)
    v3 = vb.reshape(batch, seq, DA_V)
    kern = functools.partial(_attn_kernel, tq=tq, tk=tk, layer_idx=layer_idx)
    grid_spec = pltpu.PrefetchScalarGridSpec(
        num_scalar_prefetch=2, grid=(batch, DA_HEADS, len(pairs)),
        in_specs=[
            pl.BlockSpec((1, tq, LANES), lambda b, h, p, qi, kj: (b, qi[p], h)),
            pl.BlockSpec((1, tk, LANES), lambda b, h, p, qi, kj: (b, kj[p], h)),
            pl.BlockSpec((1, tk, LANES), lambda b, h, p, qi, kj: (b, kj[p], h)),
            pl.BlockSpec((4, DA_DK), lambda b, h, p, qi, kj: (0, 0)),
            pl.BlockSpec((1, DA_DV), lambda b, h, p, qi, kj: (0, 0)),
        ],
        out_specs=pl.BlockSpec((1, tq, LANES), lambda b, h, p, qi, kj: (b, qi[p], h)),
        scratch_shapes=[pltpu.VMEM((2 * tq, LANES), BF16),
                        pltpu.VMEM((2 * tq, LANES), F32),
                        pltpu.VMEM((2 * tq, LANES), F32),
                        pltpu.VMEM((2 * tq, LANES), F32)])
    y = pl.pallas_call(
        kern, grid_spec=grid_spec,
        out_shape=jax.ShapeDtypeStruct((batch, seq, DA_V), BF16),
        compiler_params=_cparams(("parallel", "parallel", "arbitrary")),
        name="attn_prompt")(qi, kj, q3, k3, v3, lam_vec, subln_g.reshape(1, DA_DV))
    return y.reshape(batch * seq, DA_V)


def _attn_sample_kernel(pt_ref, q_ref, *refs, pp, n_steps, n_new, layer_idx):
    k_refs = refs[:pp]
    v_refs = refs[pp:2 * pp]
    kn_ref, vn_ref, lam_ref, sg_ref, o_ref, m_scr, l_scr, acc_scr = refs[2 * pp:]
    step = pl.program_id(1)
    rows = q_ref.shape[1]

    @pl.when(step == 0)
    def _():
        m_scr[...] = jnp.full(m_scr.shape, -jnp.inf, F32)
        l_scr[...] = jnp.zeros(l_scr.shape, F32)
        acc_scr[...] = jnp.zeros(acc_scr.shape, F32)

    row = lax.broadcasted_iota(jnp.int32, (rows, LANES), 0)
    lane = lax.broadcasted_iota(jnp.int32, (rows, LANES), 1)
    keep = ((row < n_new) == (lane < DA_DK)) & (row < 2 * n_new)

    def head_q(h):
        qh = q_ref[0][:, h * LANES:(h + 1) * LANES]
        return jnp.where(keep, qh, jnp.zeros_like(qh))

    for h in range(DA_HEADS):
        qz = head_q(h)
        cs = slice(h * LANES, (h + 1) * LANES)
        s_list = [lax.dot_general(qz, k_refs[u][0][:, cs].astype(BF16), NT_DIMS,
                                  preferred_element_type=F32) for u in range(pp)]
        mx = s_list[0].max(axis=-1, keepdims=True)
        for u in range(1, pp):
            mx = jnp.maximum(mx, s_list[u].max(axis=-1, keepdims=True))
        m_old = m_scr[h][:, 0:1]
        m_new = jnp.maximum(m_old, mx)
        alpha = jnp.exp(m_old - m_new)
        l_new = alpha * l_scr[h][:, 0:1]
        acc = alpha * acc_scr[h]
        for u in range(pp):
            pu = jnp.exp(s_list[u] - m_new)
            l_new = l_new + jnp.sum(pu, axis=-1, keepdims=True)
            acc = acc + jnp.dot(pu.astype(BF16), v_refs[u][0][:, cs].astype(BF16),
                                preferred_element_type=F32)
        m_scr[h] = jnp.broadcast_to(m_new, (rows, LANES))
        l_scr[h] = jnp.broadcast_to(l_new, (rows, LANES))
        acc_scr[h] = acc

    @pl.when(step == n_steps - 1)
    def _():
        lam, lam_init = _diff_lambda(lam_ref, layer_idx)
        tok = jnp.where(row >= n_new, row - n_new, row)[:, 0:1]
        for h in range(DA_HEADS):
            qf = head_q(h).astype(F32)
            cs = slice(h * LANES, (h + 1) * LANES)
            kn = kn_ref[0][:, cs]
            vn = vn_ref[0][:, cs]
            s_new = []
            for j in range(n_new):
                sj = jnp.sum(qf * kn[j:j + 1, :], axis=-1, keepdims=True)
                s_new.append(jnp.where(tok >= j, sj, -jnp.inf))
            m_old = m_scr[h][:, 0:1]
            m_new = m_old
            for sj in s_new:
                m_new = jnp.maximum(m_new, sj)
            alpha = jnp.exp(m_old - m_new)
            l_new = alpha * l_scr[h][:, 0:1]
            acc = alpha * acc_scr[h]
            for j in range(n_new):
                pj = jnp.exp(s_new[j] - m_new)
                l_new = l_new + pj
                acc = acc + pj * vn[j:j + 1, :]
            o = acc / l_new
            od = o[0:n_new, :] - lam * o[n_new:2 * n_new, :]
            od = od * lax.rsqrt(jnp.mean(od * od, axis=-1, keepdims=True) + NORM_EPS)
            o_ref[0, :, cs] = od * sg_ref[...] * (1.0 - lam_init)


def _attn_sample(qb, k_new, v_new, cache_k, cache_v, page_table, lam_vec, subln_g, *,
                 batch, n_new, layer_idx, pp=4):
    n_pages = page_table.shape[1]
    n_phys = cache_k.shape[0]
    n_steps = n_pages // pp
    q3 = qb.reshape(batch, n_new, DA_QK)
    rows = 16
    q3 = jnp.concatenate([q3, q3, jnp.zeros((batch, rows - 2 * n_new, DA_QK), q3.dtype)], axis=1)
    kc = cache_k.reshape(n_phys, PAGE_SIZE, DA_QK)
    vc = cache_v.reshape(n_phys, PAGE_SIZE, DA_V)
    kn = k_new.reshape(batch, n_new, DA_QK)
    vn = v_new.reshape(batch, n_new, DA_V)

    def page_spec(u):
        return pl.BlockSpec((1, PAGE_SIZE, DA_QK), lambda b, s, pt, u=u: (pt[b, s * pp + u], 0, 0))

    in_specs = ([pl.BlockSpec((1, rows, DA_QK), lambda b, s, pt: (b, 0, 0))]
                + [page_spec(u) for u in range(pp)] + [page_spec(u) for u in range(pp)]
                + [pl.BlockSpec((1, n_new, DA_QK), lambda b, s, pt: (b, 0, 0)),
                   pl.BlockSpec((1, n_new, DA_V), lambda b, s, pt: (b, 0, 0)),
                   pl.BlockSpec((4, DA_DK), lambda b, s, pt: (0, 0)),
                   pl.BlockSpec((1, DA_DV), lambda b, s, pt: (0, 0))])
    grid_spec = pltpu.PrefetchScalarGridSpec(
        num_scalar_prefetch=1, grid=(batch, n_steps), in_specs=in_specs,
        out_specs=pl.BlockSpec((1, n_new, DA_V), lambda b, s, pt: (b, 0, 0)),
        scratch_shapes=[pltpu.VMEM((DA_HEADS, rows, LANES), F32)] * 3)
    kern = functools.partial(_attn_sample_kernel, pp=pp, n_steps=n_steps, n_new=n_new,
                             layer_idx=layer_idx)
    y = pl.pallas_call(
        kern, grid_spec=grid_spec,
        out_shape=jax.ShapeDtypeStruct((batch, n_new, DA_V), F32),
        compiler_params=_cparams(("parallel", "arbitrary")),
        name="attn_sample")(page_table, q3, *([kc] * pp), *([vc] * pp), kn, vn, lam_vec,
                            subln_g.reshape(1, DA_DV))
    return y.reshape(batch * n_new, DA_V)


def _peer_scores_kernel(x_ref, wq_ref, sk_ref, s1_ref, s2_ref):
    q = jnp.dot(x_ref[...].astype(BF16), wq_ref[...], preferred_element_type=F32)
    for h in range(PEER_HEADS):
        for c in range(2):
            col = (h * 2 + c) * LANES
            st = lax.dot_general(sk_ref[c], q[:, col:col + LANES], NT_DIMS,
                                 precision=lax.Precision.HIGHEST, preferred_element_type=F32)
            (s1_ref if c == 0 else s2_ref)[h] = st


def _peer_scores(x, wq, sk, *, tm=512):
    t = x.shape[0]
    tm = min(tm, t)
    shp = jax.ShapeDtypeStruct((PEER_HEADS, PEER_NKEYS, t), F32)
    spec = pl.BlockSpec((PEER_HEADS, PEER_NKEYS, tm), lambda i: (0, 0, i))
    return pl.pallas_call(
        _peer_scores_kernel, grid=(t // tm,),
        in_specs=[pl.BlockSpec((tm, D_MODEL), lambda i: (i, 0)),
                  pl.BlockSpec((D_MODEL, PEER_HEADS * PEER_DKEY), lambda i: (0, 0)),
                  pl.BlockSpec((2, PEER_NKEYS, PEER_DKEY // 2), lambda i: (0, 0, 0))],
        out_specs=[spec, spec], out_shape=[shp, shp],
        compiler_params=_cparams(("parallel",)),
        name="peer_scores")(x, wq, sk)


def _peer_topk_kernel(s1_ref, s2_ref, thr_ref, a1_ref, key2_ref, a2_ref, rk1_scr):
    shp = s1_ref.shape[2:]
    neg = jnp.full(shp, -jnp.inf, F32)
    k = PEER_TOPK
    unranked = F32(k)

    def rank(s_ref, rk_ref):
        def init(n, carry):
            rk_ref[0, n] = jnp.full(shp, unranked, F32)
            return carry
        lax.fori_loop(0, PEER_NKEYS, init, 0)
        vals = []
        win = jnp.full(shp, -1.0, F32)
        for it in range(k + 1):
            def body(n, carry, it=it, win=win):
                m, mi = carry
                nf = n.astype(F32)
                r = rk_ref[0, n]
                if it > 0:
                    r = jnp.where(win == nf, F32(it - 1), r)
                    rk_ref[0, n] = r
                if it == k:
                    return carry
                w = s_ref[0, n]
                better = (r > k - 0.5) & (w > m)
                return jnp.where(better, w, m), jnp.where(better, nf, mi)
            m, mi = lax.fori_loop(0, PEER_NKEYS, body, (neg, jnp.full(shp, -1.0, F32)), unroll=4)
            if it < k:
                vals.append(m)
                win = mi
        return vals

    a = rank(s1_ref, rk1_scr)
    b = rank(s2_ref, key2_ref)

    pairs = [(i, j) for i in range(k) for j in range(k) if (i + 1) * (j + 1) <= k]
    cv = [a[i] + b[j] for i, j in pairs]
    sel = [jnp.zeros(shp, F32) for _ in pairs]
    c0 = None
    z = jnp.zeros(shp, F32)
    for it in range(k):
        m = cv[0]
        for c in cv[1:]:
            m = jnp.maximum(m, c)
        if it == 0:
            c0 = m
        z = z + jnp.exp(m - c0)
        found = jnp.zeros(shp, F32)
        for idx in range(len(pairs)):
            hit = (cv[idx] == m) & (found < 0.5)
            sel[idx] = jnp.where(hit, 1.0, sel[idx])
            cv[idx] = jnp.where(hit, neg, cv[idx])
            found = jnp.where(hit, 1.0, found)
    cnt = [jnp.zeros(shp, F32) for _ in range(k)]
    for idx, (i, j) in enumerate(pairs):
        cnt[i] = cnt[i] + sel[idx]
    inv_z = 1.0 / z

    def emit(n, carry):
        r1 = rk1_scr[0, n]
        taken = jnp.zeros(shp, F32)
        for i in range(k):
            taken = jnp.where(r1 == F32(i), cnt[i], taken)
        thr_ref[0, n] = taken - 0.5
        a1_ref[0, n] = jnp.exp(s1_ref[0, n] - a[0]) * inv_z
        a2_ref[0, n] = jnp.exp(s2_ref[0, n] - b[0])
        return carry
    lax.fori_loop(0, PEER_NKEYS, emit, 0)


def _peer_topk(s1, s2):
    t = s1.shape[2]
    nt = t // LANES
    ts = min(8, nt)
    s1v = s1.reshape(PEER_HEADS, PEER_NKEYS, nt, LANES)
    s2v = s2.reshape(PEER_HEADS, PEER_NKEYS, nt, LANES)
    spec = pl.BlockSpec((1, PEER_NKEYS, ts, LANES), lambda h, i: (h, 0, i, 0))
    shp = jax.ShapeDtypeStruct(s1v.shape, F32)
    thr, a1, key2, a2 = pl.pallas_call(
        _peer_topk_kernel, grid=(PEER_HEADS, nt // ts),
        in_specs=[spec, spec], out_specs=[spec] * 4, out_shape=[shp] * 4,
        scratch_shapes=[pltpu.VMEM((1, PEER_NKEYS, ts, LANES), F32)],
        compiler_params=_cparams(("parallel", "parallel")),
        name="peer_topk")(s1v, s2v)
    flat = (PEER_HEADS, PEER_NKEYS, t)
    row = (PEER_HEADS, PEER_NKEYS, 1, t)
    return thr.reshape(row), a1.reshape(row), key2.reshape(flat), a2.reshape(flat)


def _peer_ffn_kernel(x_ref, u_ref, vt_ref, thr_ref, a1_ref, key2_ref, a2_ref, g_ref, b_ref,
                     o_ref, xb_scr, h_scr, a_scr, acc_scr, *, n_steps, ng, cw):
    e = pl.program_id(1)
    tt = x_ref.shape[0]

    @pl.when(e == 0)
    def _():
        xb_scr[...] = x_ref[...].astype(BF16)
        acc_scr[...] = jnp.zeros(acc_scr.shape, F32)

    h_scr[...] = lax.dot_general(u_ref[...], xb_scr[...], NT_DIMS, preferred_element_type=F32)
    for gi in range(ng):
        rs = slice(gi * PEER_NKEYS, (gi + 1) * PEER_NKEYS)
        for cc in range(tt // cw):
            cs = slice(cc * cw, (cc + 1) * cw)
            wd = jnp.zeros((PEER_NKEYS, cw), F32)
            for h in range(PEER_HEADS):
                take = key2_ref[h, :, cs] < thr_ref[h, gi, :, cs]
                wd = wd + jnp.where(take, a2_ref[h, :, cs], 0.0) * a1_ref[h, gi, :, cs]
            hv = h_scr[rs, cs]
            gelu = 0.5 * hv * (1.0 + lax.erf(hv * F32(1.0 / math.sqrt(2.0))))
            a_scr[rs, cs] = (gelu * wd).astype(BF16)
    acc_scr[...] += jnp.dot(vt_ref[0], a_scr[...], preferred_element_type=F32)

    @pl.when(e == n_steps - 1)
    def _():
        y = jnp.transpose(acc_scr[...])
        o_ref[...] = _layer_norm(ALPHA * x_ref[...] + y, g_ref[...], b_ref[...])


def _peer_ffn(x, u_bf, vt_bf, thr, a1, key2, a2, g, b, *, tt=512, cw=256):
    t = x.shape[0]
    te = vt_bf.shape[2]
    tt = min(tt, t)
    cw = min(cw, tt)
    ng = te // PEER_NKEYS
    n_steps = PEER_N // te
    kern = functools.partial(_peer_ffn_kernel, n_steps=n_steps, ng=ng, cw=cw)
    return pl.pallas_call(
        kern, grid=(t // tt, n_steps),
        in_specs=[
            pl.BlockSpec((tt, D_MODEL), lambda i, e: (i, 0)),
            pl.BlockSpec((te, D_MODEL), lambda i, e: (e, 0)),
            pl.BlockSpec((1, D_MODEL, te), lambda i, e: (e, 0, 0)),
            pl.BlockSpec((PEER_HEADS, ng, 1, tt), lambda i, e: (0, e, 0, i)),
            pl.BlockSpec((PEER_HEADS, ng, 1, tt), lambda i, e: (0, e, 0, i)),
            pl.BlockSpec((PEER_HEADS, PEER_NKEYS, tt), lambda i, e: (0, 0, i)),
            pl.BlockSpec((PEER_HEADS, PEER_NKEYS, tt), lambda i, e: (0, 0, i)),
            pl.BlockSpec((1, D_MODEL), lambda i, e: (0, 0)),
            pl.BlockSpec((1, D_MODEL), lambda i, e: (0, 0)),
        ],
        out_specs=pl.BlockSpec((tt, D_MODEL), lambda i, e: (i, 0)),
        out_shape=jax.ShapeDtypeStruct((t, D_MODEL), F32),
        scratch_shapes=[pltpu.VMEM((tt, D_MODEL), BF16),
                        pltpu.VMEM((te, tt), F32),
                        pltpu.VMEM((te, tt), BF16),
                        pltpu.VMEM((D_MODEL, tt), F32)],
        compiler_params=_cparams(("parallel", "arbitrary")),
        name="peer_ffn")(x, u_bf, vt_bf, thr, a1, key2, a2,
                         g.reshape(1, D_MODEL), b.reshape(1, D_MODEL))


def _peer_layer(x, wq_bf, sk, u_bf, vt_bf, g, b):
    s1, s2 = _peer_scores(x, wq_bf, sk)
    thr, a1, key2, a2 = _peer_topk(s1, s2)
    return _peer_ffn(x, u_bf, vt_bf, thr, a1, key2, a2, g, b)


def _ret_rot_tables(pos):
    half = RET_DK // 2
    inv = 1.0 / (RET_THETA ** (jnp.arange(half, dtype=F32) / half))
    ang = pos.astype(F32)[:, None] * inv[None, :]
    return jnp.cos(ang), jnp.sin(ang)


def _da_rot_tables(pos):
    half = DA_ROT // 2
    inv = 1.0 / (ROPE_THETA ** (jnp.arange(half, dtype=F32) / half))
    ang = pos.astype(F32)[:, None] * inv[None, :]
    cos, sin = jnp.cos(ang), jnp.sin(ang)
    n = pos.shape[0]
    pad = jnp.zeros((n, DA_DK - DA_ROT), F32)
    c64 = jnp.concatenate([cos, cos, pad + 1.0], axis=1)
    lo64 = jnp.concatenate([-sin, jnp.zeros((n, half), F32), pad], axis=1)
    hi64 = jnp.concatenate([jnp.zeros((n, half), F32), sin, pad], axis=1)
    rep = LANES // DA_DK
    return jnp.tile(c64, (1, rep)), jnp.tile(lo64, (1, rep)), jnp.tile(hi64, (1, rep))


def _ret_layer(x, r0, w_in_bf, w_out_bf, rot_tabs, g, b, *, batch, seq, true_len, tm):
    qk = _proj(x, w_in_bf[:, :2 * RET_QK], tm=tm, tn=RET_DK, out_dtypes=(F32,), rot="full",
               tabs=rot_tabs, scales=(1.0, RET_DK ** -0.5), n_scale_split=RET_HEADS)[0]
    vg = _proj(x, w_in_bf[:, 2 * RET_QK:], tm=tm, tn=512, out_dtypes=(F32,))[0]
    if seq != true_len:
        pad = ((0, 0), (0, seq - true_len), (0, 0))
        qk = jnp.pad(qk.reshape(batch, true_len, -1), pad).reshape(batch * seq, -1)
        vg = jnp.pad(vg.reshape(batch, true_len, -1), pad).reshape(batch * seq, -1)
    y, r_fin = _retention(qk, vg, r0, batch=batch, seq=seq, chunk=RET_CHUNK, true_len=min(true_len, RET_CHUNK))
    if seq != true_len:
        y = y.reshape(batch, seq, RET_V)[:, :true_len].reshape(batch * true_len, RET_V)
    return _out_ln(y, w_out_bf, x, g, b, tm=tm), r_fin


def _da_project(x, w_in_bf, rot_tabs, *, tm):
    q = _proj(x, w_in_bf[:, :DA_QK], tm=tm, tn=512, out_dtypes=(BF16,), rot="partial",
              tabs=rot_tabs, scales=(DA_DK ** -0.5, DA_DK ** -0.5), n_scale_split=0)[0]
    k, kb = _proj(x, w_in_bf[:, DA_QK:2 * DA_QK], tm=tm, tn=512, out_dtypes=(F32, BF16),
                  rot="partial", tabs=rot_tabs)
    v, vb = _proj(x, w_in_bf[:, 2 * DA_QK:], tm=tm, tn=512, out_dtypes=(F32, BF16))
    return q, k, kb, v, vb


def kernel(x_prompt, x_sample, state_ret_l0, cache_k_l1, cache_v_l1, state_ret_l2, cache_k_l3,
           cache_v_l3, page_table, ret_w_in, ret_w_out, da_w_in, da_w_out, da_lambda, da_subln_g,
           peer_w_q, peer_sub_keys, peer_u, peer_v, ln_g, ln_b):
    bp, sp, _ = x_prompt.shape
    bs, ts, _ = x_sample.shape
    xp = x_prompt.reshape(bp * sp, D_MODEL)
    xs = x_sample.reshape(bs * ts, D_MODEL)
    ret_states = [state_ret_l0, state_ret_l2]
    da_caches = [(cache_k_l1, cache_v_l1), (cache_k_l3, cache_v_l3)]

    pos_p = jnp.arange(sp)
    pos_s = jnp.tile(PAST_LEN + jnp.arange(ts), bs)
    ret_tab_p, ret_tab_s = _ret_rot_tables(pos_p), _ret_rot_tables(pos_s)
    da_tab_p, da_tab_s = _da_rot_tables(pos_p), _da_rot_tables(pos_s)
    tm_p, tm_s = 512, bs * ts
    te = 512

    new_states = []
    for i in range(DEPTH):
        j = i // 2
        if i % 2 == 0:
            w_in = ret_w_in[j].astype(BF16)
            w_out = ret_w_out[j].astype(BF16)
            r0p = jnp.zeros((bp, RET_HEADS, RET_DK, RET_DV), F32)
            xp, rp = _ret_layer(xp, r0p, w_in, w_out, ret_tab_p, ln_g[i, 0], ln_b[i, 0],
                                batch=bp, seq=sp, true_len=sp, tm=tm_p)
            xs, rs = _ret_layer(xs, ret_states[j], w_in, w_out, ret_tab_s, ln_g[i, 0], ln_b[i, 0],
                                batch=bs, seq=RET_CHUNK, true_len=ts, tm=tm_s)
            new_states.append((rp, rs))
        else:
            w_in = da_w_in[j].astype(BF16)
            w_out = da_w_out[j].astype(BF16)
            ck, cv = da_caches[j]
            qp, kp, kpb, vp, vpb = _da_project(xp, w_in, da_tab_p, tm=tm_p)
            yp = _attn_prompt(qp, kpb, vpb, da_lambda[j], da_subln_g[j], batch=bp, seq=sp,
                              layer_idx=i)
            qs, ks, _, vs, _ = _da_project(xs, w_in, da_tab_s, tm=tm_s)
            ys = _attn_sample(qs, ks, vs, ck, cv, page_table, da_lambda[j], da_subln_g[j],
                              batch=bs, n_new=ts, layer_idx=i)
            xp = _out_ln(yp, w_out, xp, ln_g[i, 0], ln_b[i, 0], tm=tm_p)
            xs = _out_ln(ys, w_out, xs, ln_g[i, 0], ln_b[i, 0], tm=tm_s)
            new_states.append((kp.reshape(bp, sp, DA_HEADS, 2, DA_DK),
                               vp.reshape(bp, sp, DA_HEADS, DA_DV),
                               ks.reshape(bs, ts, DA_HEADS, 2, DA_DK),
                               vs.reshape(bs, ts, DA_HEADS, DA_DV)))
        wq = peer_w_q[i].astype(BF16)
        u_bf = peer_u[i].astype(BF16)
        vt_bf = peer_v[i].astype(BF16).reshape(PEER_N // te, te, D_MODEL).transpose(0, 2, 1)
        sk = peer_sub_keys[i]
        xp = _peer_layer(xp, wq, sk, u_bf, vt_bf, ln_g[i, 1], ln_b[i, 1])
        xs = _peer_layer(xs, wq, sk, u_bf, vt_bf, ln_g[i, 1], ln_b[i, 1])

    (rp0, rs0), (kp1, vp1, ks1, vs1), (rp2, rs2), (kp3, vp3, ks3, vs3) = new_states
    return (xp.reshape(bp, sp, D_MODEL), xs.reshape(bs, ts, D_MODEL), rp0, rs0, kp1, vp1, ks1, vs1,
            rp2, rs2, kp3, vp3, ks3, vs3)
```

```python
import functools
import math

import jax
import jax.numpy as jnp
from jax import lax
from jax.experimental import pallas as pl
from jax.experimental.pallas import tpu as pltpu

F32 = jnp.float32
BF16 = jnp.bfloat16

D_MODEL = 1024
DEPTH = 4
PAST_LEN = 8192
PAGE_SIZE = 128

RET_HEADS = 4
RET_DK = 256
RET_DV = 512
RET_QK = RET_HEADS * RET_DK
RET_V = RET_HEADS * RET_DV
RET_CHUNK = 128
RET_THETA = 10000.0

DA_HEADS = 8
DA_DK = 64
DA_DV = 128
DA_QK = DA_HEADS * 2 * DA_DK
DA_V = DA_HEADS * DA_DV
DA_ROT = DA_DK // 4
ROPE_THETA = 500000.0

PEER_HEADS = 8
PEER_NKEYS = 128
PEER_N = PEER_NKEYS * PEER_NKEYS
PEER_DKEY = 256
PEER_TOPK = 16

LN_EPS = 1e-5
NORM_EPS = 1e-5
ALPHA = (2 * DEPTH) ** 0.25

LANES = 128
VMEM_LIMIT = 56 * 1024 * 1024

NT_DIMS = (((1,), (1,)), ((), ()))
TN_DIMS = (((0,), (0,)), ((), ()))


def _cparams(sem):
    return pltpu.CompilerParams(dimension_semantics=sem, vmem_limit_bytes=VMEM_LIMIT)


def _layer_norm(z, g, b):
    mu = jnp.mean(z, axis=-1, keepdims=True)
    d = z - mu
    var = jnp.mean(d * d, axis=-1, keepdims=True)
    return d * lax.rsqrt(var + LN_EPS) * g + b


def _proj_kernel(*refs, rot, n_tab, scales, n_scale_split):
    x_ref, w_ref = refs[0], refs[1]
    tabs = refs[2:2 + n_tab]
    outs = refs[2 + n_tab:]
    acc = jnp.dot(x_ref[...].astype(BF16), w_ref[...], preferred_element_type=F32)
    tn = acc.shape[1]
    if rot == "full":
        cos, sin = tabs[0][...], tabs[1][...]
        x1, x2 = acc[:, :LANES], acc[:, LANES:]
        acc = jnp.concatenate([x1 * cos - x2 * sin, x2 * cos + x1 * sin], axis=1)
    elif rot == "partial":
        c, s_lo, s_hi = tabs[0][...], tabs[1][...], tabs[2][...]
        blocks = []
        for cb in range(tn // LANES):
            blk = acc[:, cb * LANES:(cb + 1) * LANES]
            nxt = pltpu.roll(blk, LANES - DA_ROT // 2, 1)
            prv = pltpu.roll(blk, DA_ROT // 2, 1)
            blocks.append(blk * c + nxt * s_lo + prv * s_hi)
        acc = jnp.concatenate(blocks, axis=1) if len(blocks) > 1 else blocks[0]
    if scales is not None:
        j = pl.program_id(1)
        acc = acc * jnp.where(j < n_scale_split, F32(scales[0]), F32(scales[1]))
    for o in outs:
        o[...] = acc.astype(o.dtype)


def _proj(x, w, *, tm, tn, out_dtypes, rot=None, tabs=(), scales=None, n_scale_split=0):
    t, k = x.shape
    n = w.shape[1]
    tm = min(tm, t)
    grid = (t // tm, n // tn)
    in_specs = [pl.BlockSpec((tm, k), lambda i, j: (i, 0)),
                pl.BlockSpec((k, tn), lambda i, j: (0, j))]
    for tab in tabs:
        nrb = tab.shape[0] // tm
        in_specs.append(pl.BlockSpec((tm, LANES), lambda i, j, nrb=nrb: (i % nrb, 0)))
    out_shape = [jax.ShapeDtypeStruct((t, n), dt) for dt in out_dtypes]
    out_specs = [pl.BlockSpec((tm, tn), lambda i, j: (i, j)) for _ in out_dtypes]
    kern = functools.partial(_proj_kernel, rot=rot, n_tab=len(tabs), scales=scales,
                             n_scale_split=n_scale_split)
    return pl.pallas_call(kern, grid=grid, in_specs=in_specs, out_specs=out_specs,
                          out_shape=out_shape,
                          compiler_params=_cparams(("parallel", "arbitrary")),
                          name="proj")(x, w, *tabs)


def _proj_t_kernel(wt_ref, x_ref, o_ref):
    o_ref[...] = lax.dot_general(wt_ref[...], x_ref[...].astype(BF16), NT_DIMS,
                                 preferred_element_type=F32).astype(o_ref.dtype)


def _proj_t(x, wt, *, tm, out_dtype):
    t, k = x.shape
    n = wt.shape[0]
    tm = min(tm, t)
    return pl.pallas_call(
        _proj_t_kernel, grid=(t // tm,),
        in_specs=[pl.BlockSpec((n, k), lambda i: (0, 0)),
                  pl.BlockSpec((tm, k), lambda i: (i, 0))],
        out_specs=pl.BlockSpec((n, tm), lambda i: (0, i)),
        out_shape=jax.ShapeDtypeStruct((n, t), out_dtype),
        compiler_params=_cparams(("parallel",)),
        name="proj_t")(wt, x)


def _out_ln_kernel(y_ref, w_ref, x_ref, g_ref, b_ref, o_ref, *, y_transposed):
    dims = TN_DIMS if y_transposed else (((1,), (0,)), ((), ()))
    m = lax.dot_general(y_ref[...].astype(BF16), w_ref[...], dims, preferred_element_type=F32)
    o_ref[...] = _layer_norm(ALPHA * x_ref[...] + m, g_ref[...], b_ref[...])


def _out_ln(y, w, x, g, b, *, tm=512, y_transposed=False):
    k, n = w.shape
    t = x.shape[0]
    tm = min(tm, t)
    y_spec = (pl.BlockSpec((k, tm), lambda i: (0, i)) if y_transposed
              else pl.BlockSpec((tm, k), lambda i: (i, 0)))
    return pl.pallas_call(
        functools.partial(_out_ln_kernel, y_transposed=y_transposed), grid=(t // tm,),
        in_specs=[y_spec,
                  pl.BlockSpec((k, n), lambda i: (0, 0)),
                  pl.BlockSpec((tm, n), lambda i: (i, 0)),
                  pl.BlockSpec((1, n), lambda i: (0, 0)),
                  pl.BlockSpec((1, n), lambda i: (0, 0))],
        out_specs=pl.BlockSpec((tm, n), lambda i: (i, 0)),
        out_shape=jax.ShapeDtypeStruct((t, n), F32),
        compiler_params=_cparams(("parallel",)),
        name="out_ln")(y, w, x, g.reshape(1, n), b.reshape(1, n))


def _ret_kernel(q_ref, k_ref, v_ref, g_ref, r0_ref, dm_ref, qd_ref, kd_ref, gl_ref,
                y_ref, rout_ref, r_scr, *, nc):
    c = pl.program_id(2)

    @pl.when(c == 0)
    def _():
        r_scr[...] = r0_ref[0, 0]

    q = q_ref[0]
    k = k_ref[0]
    vb = v_ref[0].astype(BF16)
    qb = q.astype(BF16)
    qk = lax.dot_general(qb, k.astype(BF16), NT_DIMS, preferred_element_type=F32)
    inner = (qk * dm_ref[0]).astype(BF16)
    r = r_scr[...]
    o = jnp.dot(inner, vb, preferred_element_type=F32)
    o = o + jnp.dot(qb, r.astype(BF16), preferred_element_type=F32) * qd_ref[0]
    kdec = (k * kd_ref[0]).astype(BF16)
    r_new = gl_ref[0] * r + lax.dot_general(kdec, vb, TN_DIMS, preferred_element_type=F32)
    r_scr[...] = r_new

    mu = jnp.mean(o, axis=-1, keepdims=True)
    d = o - mu
    var = jnp.mean(d * d, axis=-1, keepdims=True)
    on = d * lax.rsqrt(var + NORM_EPS)
    gg = g_ref[0]
    y_ref[0] = (gg * jax.nn.sigmoid(gg) * on).astype(y_ref.dtype)

    @pl.when(c == nc - 1)
    def _():
        rout_ref[0, 0] = r_new


def _ret_decay_tables(chunk, true_len):
    lg = jnp.log(1.0 - 2.0 ** (-5.0 - jnp.arange(RET_HEADS, dtype=F32)))
    idx = jnp.arange(chunk, dtype=F32)
    diff = idx[:, None] - idx[None, :]
    dm = jnp.where(diff >= 0, jnp.exp(lg[:, None, None] * jnp.maximum(diff, 0.0)), 0.0)
    qd = jnp.exp(lg[:, None] * (idx[None, :] + 1.0))[:, :, None]
    kd = jnp.where(idx[None, :] < true_len,
                   jnp.exp(lg[:, None] * (true_len - 1.0 - idx[None, :])), 0.0)[:, :, None]
    gl = jnp.exp(lg * true_len)[:, None, None]
    return dm.astype(F32), qd.astype(F32), kd.astype(F32), gl.astype(F32)


def _retention(qk, vg, r0, *, batch, seq, chunk, true_len):
    nc = seq // chunk
    qk3 = qk.reshape(batch, seq, 2 * RET_QK)
    vg3 = vg.reshape(batch, seq, 2 * RET_V)
    dm, qd, kd, gl = _ret_decay_tables(chunk, true_len)
    kern = functools.partial(_ret_kernel, nc=nc)
    y, r_fin = pl.pallas_call(
        kern, grid=(batch, RET_HEADS, nc),
        in_specs=[
            pl.BlockSpec((1, chunk, RET_DK), lambda b, h, c: (b, c, h)),
            pl.BlockSpec((1, chunk, RET_DK), lambda b, h, c: (b, c, RET_HEADS + h)),
            pl.BlockSpec((1, chunk, RET_DV), lambda b, h, c: (b, c, h)),
            pl.BlockSpec((1, chunk, RET_DV), lambda b, h, c: (b, c, RET_HEADS + h)),
            pl.BlockSpec((1, 1, RET_DK, RET_DV), lambda b, h, c: (b, h, 0, 0)),
            pl.BlockSpec((1, chunk, chunk), lambda b, h, c: (h, 0, 0)),
            pl.BlockSpec((1, chunk, 1), lambda b, h, c: (h, 0, 0)),
            pl.BlockSpec((1, chunk, 1), lambda b, h, c: (h, 0, 0)),
            pl.BlockSpec((1, 1, 1), lambda b, h, c: (h, 0, 0)),
        ],
        out_specs=[
            pl.BlockSpec((1, chunk, RET_DV), lambda b, h, c: (b, c, h)),
            pl.BlockSpec((1, 1, RET_DK, RET_DV), lambda b, h, c: (b, h, 0, 0)),
        ],
        out_shape=[jax.ShapeDtypeStruct((batch, seq, RET_V), BF16),
                   jax.ShapeDtypeStruct((batch, RET_HEADS, RET_DK, RET_DV), F32)],
        scratch_shapes=[pltpu.VMEM((RET_DK, RET_DV), F32)],
        compiler_params=_cparams(("parallel", "parallel", "arbitrary")),
        name="retention")(qk3, qk3, vg3, vg3, r0, dm, qd, kd, gl)
    return y.reshape(batch * seq, RET_V), r_fin


def _diff_lambda(lam_ref, layer_idx):
    lv = lam_ref[...]
    lam_init = 0.8 - 0.6 * math.exp(-0.3 * layer_idx)
    t1 = jnp.sum(lv[0:1, :] * lv[1:2, :], axis=-1, keepdims=True)
    t2 = jnp.sum(lv[2:3, :] * lv[3:4, :], axis=-1, keepdims=True)
    return jnp.exp(t1) - jnp.exp(t2) + lam_init, lam_init


def _attn_kernel(qi_ref, kj_ref, q_ref, k_ref, vt_ref, lam_ref, sg_ref, o_ref,
                 qz_scr, m_scr, l_scr, acc_scr, *, tq, tk, kw, layer_idx):
    p = pl.program_id(2)
    qi = qi_ref[p]
    kj = kj_ref[p]
    last_kj = (qi * tq + tq - 1) // tk

    @pl.when(kj == 0)
    def _():
        q = q_ref[0]
        lane = lax.broadcasted_iota(jnp.int32, q.shape, 1)
        zero = jnp.zeros_like(q)
        qz_scr[0:tq, :] = jnp.where(lane < DA_DK, q, zero)
        qz_scr[tq:2 * tq, :] = jnp.where(lane >= DA_DK, q, zero)
        m_scr[...] = jnp.full(m_scr.shape, -jnp.inf, F32)
        l_scr[...] = jnp.zeros(l_scr.shape, F32)
        acc_scr[...] = jnp.zeros(acc_scr.shape, F32)

    def process(masked):
        m_run = m_scr[...]
        l_run = l_scr[...]
        acc = acc_scr[...]
        for sb in range(tk // kw):
            s = lax.dot_general(k_ref[0, sb * kw:(sb + 1) * kw, :], qz_scr[...], NT_DIMS,
                                preferred_element_type=F32)
            if masked:
                row = lax.broadcasted_iota(jnp.int32, s.shape, 0)
                col = lax.broadcasted_iota(jnp.int32, s.shape, 1)
                kpos = kj * tk + sb * kw + row
                qpos = qi * tq + jnp.where(col >= tq, col - tq, col)
                s = jnp.where(kpos <= qpos, s, -jnp.inf)
            m_new = jnp.maximum(m_run, jnp.max(s, axis=0, keepdims=True))
            alpha = jnp.exp(m_run - m_new)
            pm = jnp.exp(s - m_new)
            l_run = alpha * l_run + jnp.sum(pm, axis=0, keepdims=True)
            acc = alpha * acc + jnp.dot(vt_ref[:, sb * kw:(sb + 1) * kw], pm.astype(BF16),
                                        preferred_element_type=F32)
            m_run = m_new
        m_scr[...] = m_run
        l_scr[...] = l_run
        acc_scr[...] = acc

    @pl.when(kj < last_kj)
    def _():
        process(False)

    @pl.when(kj == last_kj)
    def _():
        process(True)
        lam, lam_init = _diff_lambda(lam_ref, layer_idx)
        o = acc_scr[...] / l_scr[...]
        od = o[:, 0:tq] - lam * o[:, tq:2 * tq]
        od = od * lax.rsqrt(jnp.mean(od * od, axis=0, keepdims=True) + NORM_EPS)
        o_ref[...] = (od * sg_ref[...] * (1.0 - lam_init)).astype(o_ref.dtype)


def _attn_prompt(qb, kb, vt, lam_vec, subln_g, *, batch, seq, layer_idx, tq=512, tk=1024, kw=512):
    tq = min(tq, seq)
    tk = min(tk, seq)
    kw = min(kw, tk)
    nq, nk = seq // tq, seq // tk
    pairs = [(i, j) for i in range(nq) for j in range((i * tq + tq - 1) // tk + 1)]
    qi = jnp.asarray([pq for pq, _ in pairs], jnp.int32)
    kj = jnp.asarray([pk for _, pk in pairs], jnp.int32)
    q3 = qb.reshape(batch, seq, DA_QK)
    k3 = kb.reshape(batch, seq, DA_QK)
    kern = functools.partial(_attn_kernel, tq=tq, tk=tk, kw=kw, layer_idx=layer_idx)
    grid_spec = pltpu.PrefetchScalarGridSpec(
        num_scalar_prefetch=2, grid=(batch, DA_HEADS, len(pairs)),
        in_specs=[
            pl.BlockSpec((1, tq, LANES), lambda b, h, p, qi, kj: (b, qi[p], h)),
            pl.BlockSpec((1, tk, LANES), lambda b, h, p, qi, kj: (b, kj[p], h)),
            pl.BlockSpec((DA_DV, tk), lambda b, h, p, qi, kj: (h, b * nk + kj[p])),
            pl.BlockSpec((4, DA_DK), lambda b, h, p, qi, kj: (0, 0)),
            pl.BlockSpec((DA_DV, 1), lambda b, h, p, qi, kj: (0, 0)),
        ],
        out_specs=pl.BlockSpec((DA_DV, tq), lambda b, h, p, qi, kj: (h, b * nq + qi[p])),
        scratch_shapes=[pltpu.VMEM((2 * tq, LANES), BF16),
                        pltpu.VMEM((1, 2 * tq), F32),
                        pltpu.VMEM((1, 2 * tq), F32),
                        pltpu.VMEM((DA_DV, 2 * tq), F32)])
    return pl.pallas_call(
        kern, grid_spec=grid_spec,
        out_shape=jax.ShapeDtypeStruct((DA_V, batch * seq), BF16),
        compiler_params=_cparams(("parallel", "parallel", "arbitrary")),
        name="attn_prompt")(qi, kj, q3, k3, vt, lam_vec, subln_g.reshape(DA_DV, 1))


def _attn_sample_kernel(pt_ref, q_ref, *refs, pp, n_steps, n_new, layer_idx):
    k_refs = refs[:pp]
    v_refs = refs[pp:2 * pp]
    kn_ref, vn_ref, lam_ref, sg_ref, o_ref, m_scr, l_scr, acc_scr = refs[2 * pp:]
    step = pl.program_id(1)
    rows = q_ref.shape[1]

    @pl.when(step == 0)
    def _():
        m_scr[...] = jnp.full(m_scr.shape, -jnp.inf, F32)
        l_scr[...] = jnp.zeros(l_scr.shape, F32)
        acc_scr[...] = jnp.zeros(acc_scr.shape, F32)

    row = lax.broadcasted_iota(jnp.int32, (rows, LANES), 0)
    lane = lax.broadcasted_iota(jnp.int32, (rows, LANES), 1)
    keep = ((row < n_new) == (lane < DA_DK)) & (row < 2 * n_new)

    def head_q(h):
        qh = q_ref[0][:, h * LANES:(h + 1) * LANES]
        return jnp.where(keep, qh, jnp.zeros_like(qh))

    for h in range(DA_HEADS):
        qz = head_q(h)
        cs = slice(h * LANES, (h + 1) * LANES)
        s_list = [lax.dot_general(qz, k_refs[u][0][:, cs].astype(BF16), NT_DIMS,
                                  preferred_element_type=F32) for u in range(pp)]
        mx = s_list[0].max(axis=-1, keepdims=True)
        for u in range(1, pp):
            mx = jnp.maximum(mx, s_list[u].max(axis=-1, keepdims=True))
        m_old = m_scr[h][:, 0:1]
        m_new = jnp.maximum(m_old, mx)
        alpha = jnp.exp(m_old - m_new)
        l_new = alpha * l_scr[h][:, 0:1]
        acc = alpha * acc_scr[h]
        for u in range(pp):
            pu = jnp.exp(s_list[u] - m_new)
            l_new = l_new + jnp.sum(pu, axis=-1, keepdims=True)
            acc = acc + jnp.dot(pu.astype(BF16), v_refs[u][0][:, cs].astype(BF16),
                                preferred_element_type=F32)
        m_scr[h] = jnp.broadcast_to(m_new, (rows, LANES))
        l_scr[h] = jnp.broadcast_to(l_new, (rows, LANES))
        acc_scr[h] = acc

    @pl.when(step == n_steps - 1)
    def _():
        lam, lam_init = _diff_lambda(lam_ref, layer_idx)
        tok = jnp.where(row >= n_new, row - n_new, row)[:, 0:1]
        for h in range(DA_HEADS):
            qf = head_q(h).astype(F32)
            cs = slice(h * LANES, (h + 1) * LANES)
            kn = kn_ref[0][:, cs]
            vn = vn_ref[0][:, cs]
            s_new = []
            for j in range(n_new):
                sj = jnp.sum(qf * kn[j:j + 1, :], axis=-1, keepdims=True)
                s_new.append(jnp.where(tok >= j, sj, -jnp.inf))
            m_old = m_scr[h][:, 0:1]
            m_new = m_old
            for sj in s_new:
                m_new = jnp.maximum(m_new, sj)
            alpha = jnp.exp(m_old - m_new)
            l_new = alpha * l_scr[h][:, 0:1]
            acc = alpha * acc_scr[h]
            for j in range(n_new):
                pj = jnp.exp(s_new[j] - m_new)
                l_new = l_new + pj
                acc = acc + pj * vn[j:j + 1, :]
            o = acc / l_new
            od = o[0:n_new, :] - lam * o[n_new:2 * n_new, :]
            od = od * lax.rsqrt(jnp.mean(od * od, axis=-1, keepdims=True) + NORM_EPS)
            o_ref[0, :, cs] = od * sg_ref[...] * (1.0 - lam_init)


def _attn_sample(qb, k_new, v_new, cache_k, cache_v, page_table, lam_vec, subln_g, *,
                 batch, n_new, layer_idx, pp=4):
    n_pages = page_table.shape[1]
    n_phys = cache_k.shape[0]
    n_steps = n_pages // pp
    q3 = qb.reshape(batch, n_new, DA_QK)
    rows = 16
    q3 = jnp.concatenate([q3, q3, jnp.zeros((batch, rows - 2 * n_new, DA_QK), q3.dtype)], axis=1)
    kc = cache_k.reshape(n_phys, PAGE_SIZE, DA_QK)
    vc = cache_v.reshape(n_phys, PAGE_SIZE, DA_V)
    kn = k_new.reshape(batch, n_new, DA_QK)
    vn = v_new.reshape(batch, n_new, DA_V)

    def page_spec(u):
        return pl.BlockSpec((1, PAGE_SIZE, DA_QK), lambda b, s, pt, u=u: (pt[b, s * pp + u], 0, 0))

    in_specs = ([pl.BlockSpec((1, rows, DA_QK), lambda b, s, pt: (b, 0, 0))]
                + [page_spec(u) for u in range(pp)] + [page_spec(u) for u in range(pp)]
                + [pl.BlockSpec((1, n_new, DA_QK), lambda b, s, pt: (b, 0, 0)),
                   pl.BlockSpec((1, n_new, DA_V), lambda b, s, pt: (b, 0, 0)),
                   pl.BlockSpec((4, DA_DK), lambda b, s, pt: (0, 0)),
                   pl.BlockSpec((1, DA_DV), lambda b, s, pt: (0, 0))])
    grid_spec = pltpu.PrefetchScalarGridSpec(
        num_scalar_prefetch=1, grid=(batch, n_steps), in_specs=in_specs,
        out_specs=pl.BlockSpec((1, n_new, DA_V), lambda b, s, pt: (b, 0, 0)),
        scratch_shapes=[pltpu.VMEM((DA_HEADS, rows, LANES), F32)] * 3)
    kern = functools.partial(_attn_sample_kernel, pp=pp, n_steps=n_steps, n_new=n_new,
                             layer_idx=layer_idx)
    y = pl.pallas_call(
        kern, grid_spec=grid_spec,
        out_shape=jax.ShapeDtypeStruct((batch, n_new, DA_V), F32),
        compiler_params=_cparams(("parallel", "arbitrary")),
        name="attn_sample")(page_table, q3, *([kc] * pp), *([vc] * pp), kn, vn, lam_vec,
                            subln_g.reshape(1, DA_DV))
    return y.reshape(batch * n_new, DA_V)


def _peer_scores_kernel(x_ref, wq_ref, sk_ref, s1_ref, s2_ref):
    q = jnp.dot(x_ref[...].astype(BF16), wq_ref[...], preferred_element_type=F32)
    for h in range(PEER_HEADS):
        for c in range(2):
            col = (h * 2 + c) * LANES
            st = lax.dot_general(sk_ref[c], q[:, col:col + LANES], NT_DIMS,
                                 precision=lax.Precision.HIGHEST, preferred_element_type=F32)
            (s1_ref if c == 0 else s2_ref)[h] = st


def _peer_scores(x, wq, sk, *, tm=512):
    t = x.shape[0]
    tm = min(tm, t)
    shp = jax.ShapeDtypeStruct((PEER_HEADS, PEER_NKEYS, t), F32)
    spec = pl.BlockSpec((PEER_HEADS, PEER_NKEYS, tm), lambda i: (0, 0, i))
    return pl.pallas_call(
        _peer_scores_kernel, grid=(t // tm,),
        in_specs=[pl.BlockSpec((tm, D_MODEL), lambda i: (i, 0)),
                  pl.BlockSpec((D_MODEL, PEER_HEADS * PEER_DKEY), lambda i: (0, 0)),
                  pl.BlockSpec((2, PEER_NKEYS, PEER_DKEY // 2), lambda i: (0, 0, 0))],
        out_specs=[spec, spec], out_shape=[shp, shp],
        compiler_params=_cparams(("parallel",)),
        name="peer_scores")(x, wq, sk)


def _peer_topk_kernel(s1_ref, s2_ref, thr_ref, a1_ref, key2_ref, a2_ref, rk1_scr):
    shp = s1_ref.shape[2:]
    neg = jnp.full(shp, -jnp.inf, F32)
    k = PEER_TOPK
    unranked = F32(k)

    def rank(s_ref, rk_ref):
        def init(n, carry):
            rk_ref[0, n] = jnp.full(shp, unranked, F32)
            return carry
        lax.fori_loop(0, PEER_NKEYS, init, 0)
        vals = []
        none = jnp.full(shp, -1.0, F32)
        win = none
        nch = 8
        for it in range(k + 1):
            def body(i, carry, it=it, win=win):
                ms, mis = carry
                new_ms, new_mis = [], []
                for p in range(nch):
                    n = i * nch + p
                    nf = n.astype(F32)
                    r = rk_ref[0, n]
                    if it > 0:
                        r = jnp.where(win == nf, F32(it - 1), r)
                        rk_ref[0, n] = r
                    if it < k:
                        w = s_ref[0, n]
                        better = (r > k - 0.5) & (w > ms[p])
                        new_ms.append(jnp.where(better, w, ms[p]))
                        new_mis.append(jnp.where(better, nf, mis[p]))
                return (tuple(new_ms), tuple(new_mis)) if it < k else carry
            ms, mis = lax.fori_loop(0, PEER_NKEYS // nch, body, ((neg,) * nch, (none,) * nch))
            if it < k:
                m, mi = ms[0], mis[0]
                for p in range(1, nch):
                    better = (ms[p] > m) | ((ms[p] == m) & (mis[p] < mi))
                    m = jnp.where(better, ms[p], m)
                    mi = jnp.where(better, mis[p], mi)
                vals.append(m)
                win = mi
        return vals

    a = rank(s1_ref, rk1_scr)
    b = rank(s2_ref, key2_ref)

    pairs = [(i, j) for i in range(k) for j in range(k) if (i + 1) * (j + 1) <= k]
    cv = [a[i] + b[j] for i, j in pairs]
    sel = [jnp.zeros(shp, F32) for _ in pairs]
    c0 = None
    z = jnp.zeros(shp, F32)
    for it in range(k):
        m = cv[0]
        for c in cv[1:]:
            m = jnp.maximum(m, c)
        if it == 0:
            c0 = m
        z = z + jnp.exp(m - c0)
        found = jnp.zeros(shp, F32)
        for idx in range(len(pairs)):
            hit = (cv[idx] == m) & (found < 0.5)
            sel[idx] = jnp.where(hit, 1.0, sel[idx])
            cv[idx] = jnp.where(hit, neg, cv[idx])
            found = jnp.where(hit, 1.0, found)
    cnt = [jnp.zeros(shp, F32) for _ in range(k)]
    for idx, (i, j) in enumerate(pairs):
        cnt[i] = cnt[i] + sel[idx]
    inv_z = 1.0 / z

    def emit(n, carry):
        r1 = rk1_scr[0, n]
        taken = jnp.zeros(shp, F32)
        for i in range(k):
            taken = jnp.where(r1 == F32(i), cnt[i], taken)
        thr_ref[0, n] = taken - 0.5
        a1_ref[0, n] = jnp.exp(s1_ref[0, n] - a[0]) * inv_z
        a2_ref[0, n] = jnp.exp(s2_ref[0, n] - b[0])
        return carry
    lax.fori_loop(0, PEER_NKEYS, emit, 0, unroll=4)


def _peer_topk(s1, s2):
    t = s1.shape[2]
    nt = t // LANES
    ts = min(8, nt)
    s1v = s1.reshape(PEER_HEADS, PEER_NKEYS, nt, LANES)
    s2v = s2.reshape(PEER_HEADS, PEER_NKEYS, nt, LANES)
    spec = pl.BlockSpec((1, PEER_NKEYS, ts, LANES), lambda h, i: (h, 0, i, 0))
    shp = jax.ShapeDtypeStruct(s1v.shape, F32)
    thr, a1, key2, a2 = pl.pallas_call(
        _peer_topk_kernel, grid=(PEER_HEADS, nt // ts),
        in_specs=[spec, spec], out_specs=[spec] * 4, out_shape=[shp] * 4,
        scratch_shapes=[pltpu.VMEM((1, PEER_NKEYS, ts, LANES), F32)],
        compiler_params=_cparams(("parallel", "parallel")),
        name="peer_topk")(s1v, s2v)
    flat = (PEER_HEADS, PEER_NKEYS, t)
    row = (PEER_HEADS, PEER_NKEYS, 1, t)
    return thr.reshape(row), a1.reshape(row), key2.reshape(flat), a2.reshape(flat)


def _peer_ffn_kernel(x_ref, u_ref, vt_ref, thr_ref, a1_ref, key2_ref, a2_ref, g_ref, b_ref,
                     o_ref, xb_scr, acc_scr, *sub_scr, n_steps, nsub, gsub, cw):
    e = pl.program_id(1)
    tt = x_ref.shape[0]
    h_scrs, a_scrs = sub_scr[:nsub], sub_scr[nsub:]
    sub = gsub * PEER_NKEYS

    @pl.when(e == 0)
    def _():
        xb_scr[...] = x_ref[...].astype(BF16)
        acc_scr[...] = jnp.zeros(acc_scr.shape, F32)

    for s in range(nsub):
        h_scrs[s][...] = lax.dot_general(u_ref[s * sub:(s + 1) * sub, :], xb_scr[...], NT_DIMS,
                                         preferred_element_type=F32)
    for s in range(nsub):
        for gl in range(gsub):
            gi = s * gsub + gl
            rs = slice(gl * PEER_NKEYS, (gl + 1) * PEER_NKEYS)
            for cc in range(tt // cw):
                cs = slice(cc * cw, (cc + 1) * cw)
                wd = jnp.zeros((PEER_NKEYS, cw), F32)
                for h in range(PEER_HEADS):
                    take = key2_ref[h, :, cs] < thr_ref[h, gi, :, cs]
                    wd = wd + jnp.where(take, a2_ref[h, :, cs], 0.0) * a1_ref[h, gi, :, cs]
                hv = h_scrs[s][rs, cs]
                gelu = 0.5 * hv * (1.0 + lax.erf(hv * F32(1.0 / math.sqrt(2.0))))
                a_scrs[s][rs, cs] = (gelu * wd).astype(BF16)
        acc_scr[...] += jnp.dot(vt_ref[0, :, s * sub:(s + 1) * sub], a_scrs[s][...],
                                preferred_element_type=F32)

    @pl.when(e == n_steps - 1)
    def _():
        y = jnp.transpose(acc_scr[...])
        o_ref[...] = _layer_norm(ALPHA * x_ref[...] + y, g_ref[...], b_ref[...])


def _peer_ffn(x, u_bf, vt_bf, thr, a1, key2, a2, g, b, *, tt=512, cw=128, gsub=2):
    t = x.shape[0]
    te = vt_bf.shape[2]
    tt = min(tt, t)
    cw = min(cw, tt)
    ng = te // PEER_NKEYS
    nsub = ng // gsub
    n_steps = PEER_N // te
    kern = functools.partial(_peer_ffn_kernel, n_steps=n_steps, nsub=nsub, gsub=gsub, cw=cw)
    return pl.pallas_call(
        kern, grid=(t // tt, n_steps),
        in_specs=[
            pl.BlockSpec((tt, D_MODEL), lambda i, e: (i, 0)),
            pl.BlockSpec((te, D_MODEL), lambda i, e: (e, 0)),
            pl.BlockSpec((1, D_MODEL, te), lambda i, e: (e, 0, 0)),
            pl.BlockSpec((PEER_HEADS, ng, 1, tt), lambda i, e: (0, e, 0, i)),
            pl.BlockSpec((PEER_HEADS, ng, 1, tt), lambda i, e: (0, e, 0, i)),
            pl.BlockSpec((PEER_HEADS, PEER_NKEYS, tt), lambda i, e: (0, 0, i)),
            pl.BlockSpec((PEER_HEADS, PEER_NKEYS, tt), lambda i, e: (0, 0, i)),
            pl.BlockSpec((1, D_MODEL), lambda i, e: (0, 0)),
            pl.BlockSpec((1, D_MODEL), lambda i, e: (0, 0)),
        ],
        out_specs=pl.BlockSpec((tt, D_MODEL), lambda i, e: (i, 0)),
        out_shape=jax.ShapeDtypeStruct((t, D_MODEL), F32),
        scratch_shapes=([pltpu.VMEM((tt, D_MODEL), BF16), pltpu.VMEM((D_MODEL, tt), F32)]
                        + [pltpu.VMEM((gsub * PEER_NKEYS, tt), F32)] * nsub
                        + [pltpu.VMEM((gsub * PEER_NKEYS, tt), BF16)] * nsub),
        compiler_params=_cparams(("parallel", "arbitrary")),
        name="peer_ffn")(x, u_bf, vt_bf, thr, a1, key2, a2,
                         g.reshape(1, D_MODEL), b.reshape(1, D_MODEL))


def _peer_layer(x, wq_bf, sk, u_bf, vt_bf, g, b):
    s1, s2 = _peer_scores(x, wq_bf, sk)
    thr, a1, key2, a2 = _peer_topk(s1, s2)
    return _peer_ffn(x, u_bf, vt_bf, thr, a1, key2, a2, g, b)


def _ret_rot_tables(pos):
    half = RET_DK // 2
    inv = 1.0 / (RET_THETA ** (jnp.arange(half, dtype=F32) / half))
    ang = pos.astype(F32)[:, None] * inv[None, :]
    return jnp.cos(ang), jnp.sin(ang)


def _da_rot_tables(pos):
    half = DA_ROT // 2
    inv = 1.0 / (ROPE_THETA ** (jnp.arange(half, dtype=F32) / half))
    ang = pos.astype(F32)[:, None] * inv[None, :]
    cos, sin = jnp.cos(ang), jnp.sin(ang)
    n = pos.shape[0]
    pad = jnp.zeros((n, DA_DK - DA_ROT), F32)
    c64 = jnp.concatenate([cos, cos, pad + 1.0], axis=1)
    lo64 = jnp.concatenate([-sin, jnp.zeros((n, half), F32), pad], axis=1)
    hi64 = jnp.concatenate([jnp.zeros((n, half), F32), sin, pad], axis=1)
    rep = LANES // DA_DK
    return jnp.tile(c64, (1, rep)), jnp.tile(lo64, (1, rep)), jnp.tile(hi64, (1, rep))


def _ret_layer(x, r0, w_in_bf, w_out_bf, rot_tabs, g, b, *, batch, seq, true_len, tm):
    qk = _proj(x, w_in_bf[:, :2 * RET_QK], tm=tm, tn=RET_DK, out_dtypes=(F32,), rot="full",
               tabs=rot_tabs, scales=(1.0, RET_DK ** -0.5), n_scale_split=RET_HEADS)[0]
    vg = _proj(x, w_in_bf[:, 2 * RET_QK:], tm=tm, tn=512, out_dtypes=(F32,))[0]
    if seq != true_len:
        pad = ((0, 0), (0, seq - true_len), (0, 0))
        qk = jnp.pad(qk.reshape(batch, true_len, -1), pad).reshape(batch * seq, -1)
        vg = jnp.pad(vg.reshape(batch, true_len, -1), pad).reshape(batch * seq, -1)
    y, r_fin = _retention(qk, vg, r0, batch=batch, seq=seq, chunk=RET_CHUNK, true_len=min(true_len, RET_CHUNK))
    if seq != true_len:
        y = y.reshape(batch, seq, RET_V)[:, :true_len].reshape(batch * true_len, RET_V)
    return _out_ln(y, w_out_bf, x, g, b, tm=tm), r_fin


def _da_project(x, w_in_bf, rot_tabs, *, tm):
    q = _proj(x, w_in_bf[:, :DA_QK], tm=tm, tn=512, out_dtypes=(BF16,), rot="partial",
              tabs=rot_tabs, scales=(DA_DK ** -0.5, DA_DK ** -0.5), n_scale_split=0)[0]
    k, kb = _proj(x, w_in_bf[:, DA_QK:2 * DA_QK], tm=tm, tn=512, out_dtypes=(F32, BF16),
                  rot="partial", tabs=rot_tabs)
    v = _proj(x, w_in_bf[:, 2 * DA_QK:], tm=tm, tn=512, out_dtypes=(F32,))[0]
    return q, k, kb, v


def kernel(x_prompt, x_sample, state_ret_l0, cache_k_l1, cache_v_l1, state_ret_l2, cache_k_l3,
           cache_v_l3, page_table, ret_w_in, ret_w_out, da_w_in, da_w_out, da_lambda, da_subln_g,
           peer_w_q, peer_sub_keys, peer_u, peer_v, ln_g, ln_b):
    bp, sp, _ = x_prompt.shape
    bs, ts, _ = x_sample.shape
    xp = x_prompt.reshape(bp * sp, D_MODEL)
    xs = x_sample.reshape(bs * ts, D_MODEL)
    ret_states = [state_ret_l0, state_ret_l2]
    da_caches = [(cache_k_l1, cache_v_l1), (cache_k_l3, cache_v_l3)]

    pos_p = jnp.arange(sp)
    pos_s = jnp.tile(PAST_LEN + jnp.arange(ts), bs)
    ret_tab_p, ret_tab_s = _ret_rot_tables(pos_p), _ret_rot_tables(pos_s)
    da_tab_p, da_tab_s = _da_rot_tables(pos_p), _da_rot_tables(pos_s)
    tm_p, tm_s = 512, bs * ts
    te = 1024

    new_states = []
    for i in range(DEPTH):
        j = i // 2
        if i % 2 == 0:
            w_in = ret_w_in[j].astype(BF16)
            w_out = ret_w_out[j].astype(BF16)
            r0p = jnp.zeros((bp, RET_HEADS, RET_DK, RET_DV), F32)
            xp, rp = _ret_layer(xp, r0p, w_in, w_out, ret_tab_p, ln_g[i, 0], ln_b[i, 0],
                                batch=bp, seq=sp, true_len=sp, tm=tm_p)
            xs, rs = _ret_layer(xs, ret_states[j], w_in, w_out, ret_tab_s, ln_g[i, 0], ln_b[i, 0],
                                batch=bs, seq=RET_CHUNK, true_len=ts, tm=tm_s)
            new_states.append((rp, rs))
        else:
            w_in = da_w_in[j].astype(BF16)
            w_out = da_w_out[j].astype(BF16)
            ck, cv = da_caches[j]
            qp, kp, kpb, vp = _da_project(xp, w_in, da_tab_p, tm=tm_p)
            vtp = _proj_t(xp, w_in[:, 2 * DA_QK:].T, tm=tm_p, out_dtype=BF16)
            yp = _attn_prompt(qp, kpb, vtp, da_lambda[j], da_subln_g[j], batch=bp, seq=sp,
                              layer_idx=i)
            qs, ks, _, vs = _da_project(xs, w_in, da_tab_s, tm=tm_s)
            ys = _attn_sample(qs, ks, vs, ck, cv, page_table, da_lambda[j], da_subln_g[j],
                              batch=bs, n_new=ts, layer_idx=i)
            xp = _out_ln(yp, w_out, xp, ln_g[i, 0], ln_b[i, 0], tm=tm_p, y_transposed=True)
            xs = _out_ln(ys, w_out, xs, ln_g[i, 0], ln_b[i, 0], tm=tm_s)
            new_states.append((kp.reshape(bp, sp, DA_HEADS, 2, DA_DK),
                               vp.reshape(bp, sp, DA_HEADS, DA_DV),
                               ks.reshape(bs, ts, DA_HEADS, 2, DA_DK),
                               vs.reshape(bs, ts, DA_HEADS, DA_DV)))
        wq = peer_w_q[i].astype(BF16)
        u_bf = peer_u[i].astype(BF16)
        vt_bf = peer_v[i].astype(BF16).reshape(PEER_N // te, te, D_MODEL).transpose(0, 2, 1)
        sk = peer_sub_keys[i]
        xp = _peer_layer(xp, wq, sk, u_bf, vt_bf, ln_g[i, 1], ln_b[i, 1])
        xs = _peer_layer(xs, wq, sk, u_bf, vt_bf, ln_g[i, 1], ln_b[i, 1])

    (rp0, rs0), (kp1, vp1, ks1, vs1), (rp2, rs2), (kp3, vp3, ks3, vs3) = new_states
    return (xp.reshape(bp, sp, D_MODEL), xs.reshape(bs, ts, D_MODEL), rp0, rs0, kp1, vp1, ks1, vs1,
            rp2, rs2, kp3, vp3, ks3, vs3)
```

```python
import functools
import math

import jax
import jax.numpy as jnp
from jax import lax
from jax.experimental import pallas as pl
from jax.experimental.pallas import tpu as pltpu

F32 = jnp.float32
BF16 = jnp.bfloat16

D_MODEL = 1024
DEPTH = 4
PAST_LEN = 8192
PAGE_SIZE = 128

RET_HEADS = 4
RET_DK = 256
RET_DV = 512
RET_QK = RET_HEADS * RET_DK
RET_V = RET_HEADS * RET_DV
RET_CHUNK = 128
RET_THETA = 10000.0

DA_HEADS = 8
DA_DK = 64
DA_DV = 128
DA_QK = DA_HEADS * 2 * DA_DK
DA_V = DA_HEADS * DA_DV
DA_ROT = DA_DK // 4
ROPE_THETA = 500000.0

PEER_HEADS = 8
PEER_NKEYS = 128
PEER_N = PEER_NKEYS * PEER_NKEYS
PEER_DKEY = 256
PEER_TOPK = 16

LN_EPS = 1e-5
NORM_EPS = 1e-5
ALPHA = (2 * DEPTH) ** 0.25

LANES = 128
VMEM_LIMIT = 56 * 1024 * 1024

NT_DIMS = (((1,), (1,)), ((), ()))
TN_DIMS = (((0,), (0,)), ((), ()))


def _cparams(sem):
    return pltpu.CompilerParams(dimension_semantics=sem, vmem_limit_bytes=VMEM_LIMIT)


def _layer_norm(z, g, b):
    mu = jnp.mean(z, axis=-1, keepdims=True)
    d = z - mu
    var = jnp.mean(d * d, axis=-1, keepdims=True)
    return d * lax.rsqrt(var + LN_EPS) * g + b


def _proj_kernel(*refs, rot, n_tab, scales, n_scale_split):
    x_ref, w_ref = refs[0], refs[1]
    tabs = refs[2:2 + n_tab]
    outs = refs[2 + n_tab:]
    acc = jnp.dot(x_ref[...].astype(BF16), w_ref[...], preferred_element_type=F32)
    tn = acc.shape[1]
    if rot == "full":
        cos, sin = tabs[0][...], tabs[1][...]
        x1, x2 = acc[:, :LANES], acc[:, LANES:]
        acc = jnp.concatenate([x1 * cos - x2 * sin, x2 * cos + x1 * sin], axis=1)
    elif rot == "partial":
        c, s_lo, s_hi = tabs[0][...], tabs[1][...], tabs[2][...]
        blocks = []
        for cb in range(tn // LANES):
            blk = acc[:, cb * LANES:(cb + 1) * LANES]
            nxt = pltpu.roll(blk, LANES - DA_ROT // 2, 1)
            prv = pltpu.roll(blk, DA_ROT // 2, 1)
            blocks.append(blk * c + nxt * s_lo + prv * s_hi)
        acc = jnp.concatenate(blocks, axis=1) if len(blocks) > 1 else blocks[0]
    if scales is not None:
        j = pl.program_id(1)
        acc = acc * jnp.where(j < n_scale_split, F32(scales[0]), F32(scales[1]))
    for o in outs:
        o[...] = acc.astype(o.dtype)


def _proj(x, w, *, tm, tn, out_dtypes, rot=None, tabs=(), scales=None, n_scale_split=0):
    t, k = x.shape
    n = w.shape[1]
    tm = min(tm, t)
    grid = (t // tm, n // tn)
    in_specs = [pl.BlockSpec((tm, k), lambda i, j: (i, 0)),
                pl.BlockSpec((k, tn), lambda i, j: (0, j))]
    for tab in tabs:
        nrb = tab.shape[0] // tm
        in_specs.append(pl.BlockSpec((tm, LANES), lambda i, j, nrb=nrb: (i % nrb, 0)))
    out_shape = [jax.ShapeDtypeStruct((t, n), dt) for dt in out_dtypes]
    out_specs = [pl.BlockSpec((tm, tn), lambda i, j: (i, j)) for _ in out_dtypes]
    kern = functools.partial(_proj_kernel, rot=rot, n_tab=len(tabs), scales=scales,
                             n_scale_split=n_scale_split)
    return pl.pallas_call(kern, grid=grid, in_specs=in_specs, out_specs=out_specs,
                          out_shape=out_shape,
                          compiler_params=_cparams(("parallel", "arbitrary")),
                          name="proj")(x, w, *tabs)


def _proj_t_kernel(wt_ref, x_ref, o_ref):
    o_ref[...] = lax.dot_general(wt_ref[...], x_ref[...].astype(BF16), NT_DIMS,
                                 preferred_element_type=F32).astype(o_ref.dtype)


def _proj_t(x, wt, *, tm, out_dtype):
    t, k = x.shape
    n = wt.shape[0]
    tm = min(tm, t)
    return pl.pallas_call(
        _proj_t_kernel, grid=(t // tm,),
        in_specs=[pl.BlockSpec((n, k), lambda i: (0, 0)),
                  pl.BlockSpec((tm, k), lambda i: (i, 0))],
        out_specs=pl.BlockSpec((n, tm), lambda i: (0, i)),
        out_shape=jax.ShapeDtypeStruct((n, t), out_dtype),
        compiler_params=_cparams(("parallel",)),
        name="proj_t")(wt, x)


def _out_ln_kernel(y_ref, w_ref, x_ref, g_ref, b_ref, o_ref, *, y_transposed):
    dims = TN_DIMS if y_transposed else (((1,), (0,)), ((), ()))
    m = lax.dot_general(y_ref[...].astype(BF16), w_ref[...], dims, preferred_element_type=F32)
    o_ref[...] = _layer_norm(ALPHA * x_ref[...] + m, g_ref[...], b_ref[...])


def _out_ln(y, w, x, g, b, *, tm=512, y_transposed=False):
    k, n = w.shape
    t = x.shape[0]
    tm = min(tm, t)
    y_spec = (pl.BlockSpec((k, tm), lambda i: (0, i)) if y_transposed
              else pl.BlockSpec((tm, k), lambda i: (i, 0)))
    return pl.pallas_call(
        functools.partial(_out_ln_kernel, y_transposed=y_transposed), grid=(t // tm,),
        in_specs=[y_spec,
                  pl.BlockSpec((k, n), lambda i: (0, 0)),
                  pl.BlockSpec((tm, n), lambda i: (i, 0)),
                  pl.BlockSpec((1, n), lambda i: (0, 0)),
                  pl.BlockSpec((1, n), lambda i: (0, 0))],
        out_specs=pl.BlockSpec((tm, n), lambda i: (i, 0)),
        out_shape=jax.ShapeDtypeStruct((t, n), F32),
        compiler_params=_cparams(("parallel",)),
        name="out_ln")(y, w, x, g.reshape(1, n), b.reshape(1, n))


def _ret_kernel(q_ref, k_ref, v_ref, g_ref, r0_ref, dm_ref, qd_ref, kd_ref, gl_ref,
                y_ref, rout_ref, r_scr, *, nc):
    c = pl.program_id(2)

    @pl.when(c == 0)
    def _():
        r_scr[...] = r0_ref[0, 0]

    q = q_ref[0]
    k = k_ref[0]
    vb = v_ref[0].astype(BF16)
    qb = q.astype(BF16)
    qk = lax.dot_general(qb, k.astype(BF16), NT_DIMS, preferred_element_type=F32)
    inner = (qk * dm_ref[0]).astype(BF16)
    r = r_scr[...]
    o = jnp.dot(inner, vb, preferred_element_type=F32)
    o = o + jnp.dot(qb, r.astype(BF16), preferred_element_type=F32) * qd_ref[0]
    kdec = (k * kd_ref[0]).astype(BF16)
    r_new = gl_ref[0] * r + lax.dot_general(kdec, vb, TN_DIMS, preferred_element_type=F32)
    r_scr[...] = r_new

    mu = jnp.mean(o, axis=-1, keepdims=True)
    d = o - mu
    var = jnp.mean(d * d, axis=-1, keepdims=True)
    on = d * lax.rsqrt(var + NORM_EPS)
    gg = g_ref[0]
    y_ref[0] = (gg * jax.nn.sigmoid(gg) * on).astype(y_ref.dtype)

    @pl.when(c == nc - 1)
    def _():
        rout_ref[0, 0] = r_new


def _ret_decay_tables(chunk, true_len):
    lg = jnp.log(1.0 - 2.0 ** (-5.0 - jnp.arange(RET_HEADS, dtype=F32)))
    idx = jnp.arange(chunk, dtype=F32)
    diff = idx[:, None] - idx[None, :]
    dm = jnp.where(diff >= 0, jnp.exp(lg[:, None, None] * jnp.maximum(diff, 0.0)), 0.0)
    qd = jnp.exp(lg[:, None] * (idx[None, :] + 1.0))[:, :, None]
    kd = jnp.where(idx[None, :] < true_len,
                   jnp.exp(lg[:, None] * (true_len - 1.0 - idx[None, :])), 0.0)[:, :, None]
    gl = jnp.exp(lg * true_len)[:, None, None]
    return dm.astype(F32), qd.astype(F32), kd.astype(F32), gl.astype(F32)


def _retention(qk, vg, r0, *, batch, seq, chunk, true_len):
    nc = seq // chunk
    qk3 = qk.reshape(batch, seq, 2 * RET_QK)
    vg3 = vg.reshape(batch, seq, 2 * RET_V)
    dm, qd, kd, gl = _ret_decay_tables(chunk, true_len)
    kern = functools.partial(_ret_kernel, nc=nc)
    y, r_fin = pl.pallas_call(
        kern, grid=(batch, RET_HEADS, nc),
        in_specs=[
            pl.BlockSpec((1, chunk, RET_DK), lambda b, h, c: (b, c, h)),
            pl.BlockSpec((1, chunk, RET_DK), lambda b, h, c: (b, c, RET_HEADS + h)),
            pl.BlockSpec((1, chunk, RET_DV), lambda b, h, c: (b, c, h)),
            pl.BlockSpec((1, chunk, RET_DV), lambda b, h, c: (b, c, RET_HEADS + h)),
            pl.BlockSpec((1, 1, RET_DK, RET_DV), lambda b, h, c: (b, h, 0, 0)),
            pl.BlockSpec((1, chunk, chunk), lambda b, h, c: (h, 0, 0)),
            pl.BlockSpec((1, chunk, 1), lambda b, h, c: (h, 0, 0)),
            pl.BlockSpec((1, chunk, 1), lambda b, h, c: (h, 0, 0)),
            pl.BlockSpec((1, 1, 1), lambda b, h, c: (h, 0, 0)),
        ],
        out_specs=[
            pl.BlockSpec((1, chunk, RET_DV), lambda b, h, c: (b, c, h)),
            pl.BlockSpec((1, 1, RET_DK, RET_DV), lambda b, h, c: (b, h, 0, 0)),
        ],
        out_shape=[jax.ShapeDtypeStruct((batch, seq, RET_V), BF16),
                   jax.ShapeDtypeStruct((batch, RET_HEADS, RET_DK, RET_DV), F32)],
        scratch_shapes=[pltpu.VMEM((RET_DK, RET_DV), F32)],
        compiler_params=_cparams(("parallel", "parallel", "arbitrary")),
        name="retention")(qk3, qk3, vg3, vg3, r0, dm, qd, kd, gl)
    return y.reshape(batch * seq, RET_V), r_fin


def _diff_lambda(lam_ref, layer_idx):
    lv = lam_ref[...]
    lam_init = 0.8 - 0.6 * math.exp(-0.3 * layer_idx)
    t1 = jnp.sum(lv[0:1, :] * lv[1:2, :], axis=-1, keepdims=True)
    t2 = jnp.sum(lv[2:3, :] * lv[3:4, :], axis=-1, keepdims=True)
    return jnp.exp(t1) - jnp.exp(t2) + lam_init, lam_init


def _attn_kernel(qi_ref, kj_ref, q_ref, k_ref, vt_ref, lam_ref, sg_ref, o_ref,
                 qz_scr, m_scr, l_scr, acc_scr, *, tq, tk, kw, layer_idx):
    p = pl.program_id(2)
    qi = qi_ref[p]
    kj = kj_ref[p]
    last_kj = (qi * tq + tq - 1) // tk

    @pl.when(kj == 0)
    def _():
        q = q_ref[0]
        lane = lax.broadcasted_iota(jnp.int32, q.shape, 1)
        zero = jnp.zeros_like(q)
        qz_scr[0:tq, :] = jnp.where(lane < DA_DK, q, zero)
        qz_scr[tq:2 * tq, :] = jnp.where(lane >= DA_DK, q, zero)
        m_scr[...] = jnp.full(m_scr.shape, -jnp.inf, F32)
        l_scr[...] = jnp.zeros(l_scr.shape, F32)
        acc_scr[...] = jnp.zeros(acc_scr.shape, F32)

    def process(masked):
        m_run = m_scr[...]
        l_run = l_scr[...]
        acc = acc_scr[...]
        for sb in range(tk // kw):
            s = lax.dot_general(k_ref[0, sb * kw:(sb + 1) * kw, :], qz_scr[...], NT_DIMS,
                                preferred_element_type=F32)
            if masked:
                row = lax.broadcasted_iota(jnp.int32, s.shape, 0)
                col = lax.broadcasted_iota(jnp.int32, s.shape, 1)
                kpos = kj * tk + sb * kw + row
                qpos = qi * tq + jnp.where(col >= tq, col - tq, col)
                s = jnp.where(kpos <= qpos, s, -jnp.inf)
            m_new = jnp.maximum(m_run, jnp.max(s, axis=0, keepdims=True))
            alpha = jnp.exp(m_run - m_new)
            pm = jnp.exp(s - m_new)
            l_run = alpha * l_run + jnp.sum(pm, axis=0, keepdims=True)
            acc = alpha * acc + jnp.dot(vt_ref[:, sb * kw:(sb + 1) * kw], pm.astype(BF16),
                                        preferred_element_type=F32)
            m_run = m_new
        m_scr[...] = m_run
        l_scr[...] = l_run
        acc_scr[...] = acc

    @pl.when(kj < last_kj)
    def _():
        process(False)

    @pl.when(kj == last_kj)
    def _():
        process(True)
        lam, lam_init = _diff_lambda(lam_ref, layer_idx)
        o = acc_scr[...] / l_scr[...]
        od = o[:, 0:tq] - lam * o[:, tq:2 * tq]
        od = od * lax.rsqrt(jnp.mean(od * od, axis=0, keepdims=True) + NORM_EPS)
        o_ref[...] = (od * sg_ref[...] * (1.0 - lam_init)).astype(o_ref.dtype)


def _attn_prompt(qb, kb, vt, lam_vec, subln_g, *, batch, seq, layer_idx, tq=512, tk=1024, kw=512):
    tq = min(tq, seq)
    tk = min(tk, seq)
    kw = min(kw, tk)
    nq, nk = seq // tq, seq // tk
    pairs = [(i, j) for i in range(nq) for j in range((i * tq + tq - 1) // tk + 1)]
    qi = jnp.asarray([pq for pq, _ in pairs], jnp.int32)
    kj = jnp.asarray([pk for _, pk in pairs], jnp.int32)
    q3 = qb.reshape(batch, seq, DA_QK)
    k3 = kb.reshape(batch, seq, DA_QK)
    kern = functools.partial(_attn_kernel, tq=tq, tk=tk, kw=kw, layer_idx=layer_idx)
    grid_spec = pltpu.PrefetchScalarGridSpec(
        num_scalar_prefetch=2, grid=(batch, DA_HEADS, len(pairs)),
        in_specs=[
            pl.BlockSpec((1, tq, LANES), lambda b, h, p, qi, kj: (b, qi[p], h)),
            pl.BlockSpec((1, tk, LANES), lambda b, h, p, qi, kj: (b, kj[p], h)),
            pl.BlockSpec((DA_DV, tk), lambda b, h, p, qi, kj: (h, b * nk + kj[p])),
            pl.BlockSpec((4, DA_DK), lambda b, h, p, qi, kj: (0, 0)),
            pl.BlockSpec((DA_DV, 1), lambda b, h, p, qi, kj: (0, 0)),
        ],
        out_specs=pl.BlockSpec((DA_DV, tq), lambda b, h, p, qi, kj: (h, b * nq + qi[p])),
        scratch_shapes=[pltpu.VMEM((2 * tq, LANES), BF16),
                        pltpu.VMEM((1, 2 * tq), F32),
                        pltpu.VMEM((1, 2 * tq), F32),
                        pltpu.VMEM((DA_DV, 2 * tq), F32)])
    return pl.pallas_call(
        kern, grid_spec=grid_spec,
        out_shape=jax.ShapeDtypeStruct((DA_V, batch * seq), BF16),
        compiler_params=_cparams(("parallel", "parallel", "arbitrary")),
        name="attn_prompt")(qi, kj, q3, k3, vt, lam_vec, subln_g.reshape(DA_DV, 1))


def _attn_sample_kernel(pt_ref, q_ref, *refs, pp, n_steps, n_new, layer_idx):
    k_refs = refs[:pp]
    v_refs = refs[pp:2 * pp]
    kn_ref, vn_ref, lam_ref, sg_ref, o_ref, m_scr, l_scr, acc_scr = refs[2 * pp:]
    step = pl.program_id(1)
    rows = q_ref.shape[1]

    @pl.when(step == 0)
    def _():
        m_scr[...] = jnp.full(m_scr.shape, -jnp.inf, F32)
        l_scr[...] = jnp.zeros(l_scr.shape, F32)
        acc_scr[...] = jnp.zeros(acc_scr.shape, F32)

    row = lax.broadcasted_iota(jnp.int32, (rows, LANES), 0)
    lane = lax.broadcasted_iota(jnp.int32, (rows, LANES), 1)
    keep = ((row < n_new) == (lane < DA_DK)) & (row < 2 * n_new)

    def head_q(h):
        qh = q_ref[0][:, h * LANES:(h + 1) * LANES]
        return jnp.where(keep, qh, jnp.zeros_like(qh))

    for h in range(DA_HEADS):
        qz = head_q(h)
        cs = slice(h * LANES, (h + 1) * LANES)
        s_list = [lax.dot_general(qz, k_refs[u][0][:, cs].astype(BF16), NT_DIMS,
                                  preferred_element_type=F32) for u in range(pp)]
        mx = s_list[0].max(axis=-1, keepdims=True)
        for u in range(1, pp):
            mx = jnp.maximum(mx, s_list[u].max(axis=-1, keepdims=True))
        m_old = m_scr[h][:, 0:1]
        m_new = jnp.maximum(m_old, mx)
        alpha = jnp.exp(m_old - m_new)
        l_new = alpha * l_scr[h][:, 0:1]
        acc = alpha * acc_scr[h]
        for u in range(pp):
            pu = jnp.exp(s_list[u] - m_new)
            l_new = l_new + jnp.sum(pu, axis=-1, keepdims=True)
            acc = acc + jnp.dot(pu.astype(BF16), v_refs[u][0, :, h, :].astype(BF16),
                                preferred_element_type=F32)
        m_scr[h] = jnp.broadcast_to(m_new, (rows, LANES))
        l_scr[h] = jnp.broadcast_to(l_new, (rows, LANES))
        acc_scr[h] = acc

    @pl.when(step == n_steps - 1)
    def _():
        lam, lam_init = _diff_lambda(lam_ref, layer_idx)
        tok = jnp.where(row >= n_new, row - n_new, row)[:, 0:1]
        for h in range(DA_HEADS):
            qf = head_q(h).astype(F32)
            cs = slice(h * LANES, (h + 1) * LANES)
            kn = kn_ref[0][:, cs]
            vn = vn_ref[0][:, cs]
            s_new = []
            for j in range(n_new):
                sj = jnp.sum(qf * kn[j:j + 1, :], axis=-1, keepdims=True)
                s_new.append(jnp.where(tok >= j, sj, -jnp.inf))
            m_old = m_scr[h][:, 0:1]
            m_new = m_old
            for sj in s_new:
                m_new = jnp.maximum(m_new, sj)
            alpha = jnp.exp(m_old - m_new)
            l_new = alpha * l_scr[h][:, 0:1]
            acc = alpha * acc_scr[h]
            for j in range(n_new):
                pj = jnp.exp(s_new[j] - m_new)
                l_new = l_new + pj
                acc = acc + pj * vn[j:j + 1, :]
            o = acc / l_new
            od = o[0:n_new, :] - lam * o[n_new:2 * n_new, :]
            od = od * lax.rsqrt(jnp.mean(od * od, axis=-1, keepdims=True) + NORM_EPS)
            o_ref[0, :, cs] = od * sg_ref[...] * (1.0 - lam_init)


def _attn_sample(qb, k_new, v_new, cache_k, cache_v, page_table, lam_vec, subln_g, *,
                 batch, n_new, layer_idx, pp=4):
    n_pages = page_table.shape[1]
    n_phys = cache_k.shape[0]
    n_steps = n_pages // pp
    q3 = qb.reshape(batch, n_new, DA_QK)
    rows = 16
    q3 = jnp.concatenate([q3, q3, jnp.zeros((batch, rows - 2 * n_new, DA_QK), q3.dtype)], axis=1)
    kc = cache_k.reshape(n_phys, PAGE_SIZE, DA_QK)
    vc = cache_v
    kn = k_new.reshape(batch, n_new, DA_QK)
    vn = v_new.reshape(batch, n_new, DA_V)

    def page_spec(u):
        return pl.BlockSpec((1, PAGE_SIZE, DA_QK), lambda b, s, pt, u=u: (pt[b, s * pp + u], 0, 0))

    def vpage_spec(u):
        return pl.BlockSpec((1, PAGE_SIZE, DA_HEADS, DA_DV),
                            lambda b, s, pt, u=u: (pt[b, s * pp + u], 0, 0, 0))

    in_specs = ([pl.BlockSpec((1, rows, DA_QK), lambda b, s, pt: (b, 0, 0))]
                + [page_spec(u) for u in range(pp)] + [vpage_spec(u) for u in range(pp)]
                + [pl.BlockSpec((1, n_new, DA_QK), lambda b, s, pt: (b, 0, 0)),
                   pl.BlockSpec((1, n_new, DA_V), lambda b, s, pt: (b, 0, 0)),
                   pl.BlockSpec((4, DA_DK), lambda b, s, pt: (0, 0)),
                   pl.BlockSpec((1, DA_DV), lambda b, s, pt: (0, 0))])
    grid_spec = pltpu.PrefetchScalarGridSpec(
        num_scalar_prefetch=1, grid=(batch, n_steps), in_specs=in_specs,
        out_specs=pl.BlockSpec((1, n_new, DA_V), lambda b, s, pt: (b, 0, 0)),
        scratch_shapes=[pltpu.VMEM((DA_HEADS, rows, LANES), F32)] * 3)
    kern = functools.partial(_attn_sample_kernel, pp=pp, n_steps=n_steps, n_new=n_new,
                             layer_idx=layer_idx)
    y = pl.pallas_call(
        kern, grid_spec=grid_spec,
        out_shape=jax.ShapeDtypeStruct((batch, n_new, DA_V), F32),
        compiler_params=_cparams(("parallel", "arbitrary")),
        name="attn_sample")(page_table, q3, *([kc] * pp), *([vc] * pp), kn, vn, lam_vec,
                            subln_g.reshape(1, DA_DV))
    return y.reshape(batch * n_new, DA_V)


def _peer_scores_kernel(x_ref, wq_ref, sk_ref, s1_ref, s2_ref):
    q = jnp.dot(x_ref[...].astype(BF16), wq_ref[...], preferred_element_type=F32)
    for h in range(PEER_HEADS):
        for c in range(2):
            col = (h * 2 + c) * LANES
            st = lax.dot_general(sk_ref[c], q[:, col:col + LANES], NT_DIMS,
                                 precision=lax.Precision.HIGHEST, preferred_element_type=F32)
            (s1_ref if c == 0 else s2_ref)[h] = st


def _peer_scores(x, wq, sk, *, tm=512):
    t = x.shape[0]
    tm = min(tm, t)
    shp = jax.ShapeDtypeStruct((PEER_HEADS, PEER_NKEYS, t), F32)
    spec = pl.BlockSpec((PEER_HEADS, PEER_NKEYS, tm), lambda i: (0, 0, i))
    return pl.pallas_call(
        _peer_scores_kernel, grid=(t // tm,),
        in_specs=[pl.BlockSpec((tm, D_MODEL), lambda i: (i, 0)),
                  pl.BlockSpec((D_MODEL, PEER_HEADS * PEER_DKEY), lambda i: (0, 0)),
                  pl.BlockSpec((2, PEER_NKEYS, PEER_DKEY // 2), lambda i: (0, 0, 0))],
        out_specs=[spec, spec], out_shape=[shp, shp],
        compiler_params=_cparams(("parallel",)),
        name="peer_scores")(x, wq, sk)


def _peer_topk_kernel(s1_ref, s2_ref, thr_ref, a1_ref, key2_ref, a2_ref, rk1_scr):
    shp = s1_ref.shape[2:]
    neg = jnp.full(shp, -jnp.inf, F32)
    k = PEER_TOPK
    unranked = F32(k)

    def rank(s_ref, rk_ref):
        def init(n, carry):
            rk_ref[0, n] = jnp.full(shp, unranked, F32)
            return carry
        lax.fori_loop(0, PEER_NKEYS, init, 0)
        vals = []
        none = jnp.full(shp, -1.0, F32)
        win = none
        nch = 8
        for it in range(k + 1):
            def body(i, carry, it=it, win=win):
                ms, mis = carry
                new_ms, new_mis = [], []
                for p in range(nch):
                    n = i * nch + p
                    nf = F32(n)
                    r = rk_ref[0, n]
                    if it > 0:
                        r = jnp.where(win == nf, F32(it - 1), r)
                        rk_ref[0, n] = r
                    if it < k:
                        w = s_ref[0, n]
                        better = (r > k - 0.5) & (w > ms[p])
                        new_ms.append(jnp.where(better, w, ms[p]))
                        new_mis.append(jnp.where(better, nf, mis[p]))
                return (tuple(new_ms), tuple(new_mis)) if it < k else carry
            carry = ((neg,) * nch, (none,) * nch)
            for i in range(PEER_NKEYS // nch):
                carry = body(i, carry)
            ms, mis = carry
            if it < k:
                m, mi = ms[0], mis[0]
                for p in range(1, nch):
                    better = (ms[p] > m) | ((ms[p] == m) & (mis[p] < mi))
                    m = jnp.where(better, ms[p], m)
                    mi = jnp.where(better, mis[p], mi)
                vals.append(m)
                win = mi
        return vals

    a = rank(s1_ref, rk1_scr)
    b = rank(s2_ref, key2_ref)

    pairs = [(i, j) for i in range(k) for j in range(k) if (i + 1) * (j + 1) <= k]
    cv = [a[i] + b[j] for i, j in pairs]
    sel = [jnp.zeros(shp, F32) for _ in pairs]
    c0 = None
    z = jnp.zeros(shp, F32)
    for it in range(k):
        m = cv[0]
        for c in cv[1:]:
            m = jnp.maximum(m, c)
        if it == 0:
            c0 = m
        z = z + jnp.exp(m - c0)
        found = jnp.zeros(shp, F32)
        for idx in range(len(pairs)):
            hit = (cv[idx] == m) & (found < 0.5)
            sel[idx] = jnp.where(hit, 1.0, sel[idx])
            cv[idx] = jnp.where(hit, neg, cv[idx])
            found = jnp.where(hit, 1.0, found)
    cnt = [jnp.zeros(shp, F32) for _ in range(k)]
    for idx, (i, j) in enumerate(pairs):
        cnt[i] = cnt[i] + sel[idx]
    inv_z = 1.0 / z

    def emit(n, carry):
        r1 = rk1_scr[0, n]
        taken = jnp.zeros(shp, F32)
        for i in range(k):
            taken = jnp.where(r1 == F32(i), cnt[i], taken)
        thr_ref[0, n] = taken - 0.5
        a1_ref[0, n] = jnp.exp(s1_ref[0, n] - a[0]) * inv_z
        a2_ref[0, n] = jnp.exp(s2_ref[0, n] - b[0])
        return carry
    lax.fori_loop(0, PEER_NKEYS, emit, 0, unroll=4)


def _peer_topk(s1, s2):
    t = s1.shape[2]
    nt = t // LANES
    ts = min(8, nt)
    s1v = s1.reshape(PEER_HEADS, PEER_NKEYS, nt, LANES)
    s2v = s2.reshape(PEER_HEADS, PEER_NKEYS, nt, LANES)
    spec = pl.BlockSpec((1, PEER_NKEYS, ts, LANES), lambda h, i: (h, 0, i, 0))
    shp = jax.ShapeDtypeStruct(s1v.shape, F32)
    thr, a1, key2, a2 = pl.pallas_call(
        _peer_topk_kernel, grid=(PEER_HEADS, nt // ts),
        in_specs=[spec, spec], out_specs=[spec] * 4, out_shape=[shp] * 4,
        scratch_shapes=[pltpu.VMEM((1, PEER_NKEYS, ts, LANES), F32)],
        compiler_params=_cparams(("parallel", "parallel")),
        name="peer_topk")(s1v, s2v)
    flat = (PEER_HEADS, PEER_NKEYS, t)
    row = (PEER_HEADS, PEER_NKEYS, 1, t)
    return thr.reshape(row), a1.reshape(row), key2.reshape(flat), a2.reshape(flat)


def _peer_ffn_kernel(x_ref, u_ref, vt_ref, thr_ref, a1_ref, key2_ref, a2_ref, g_ref, b_ref,
                     o_ref, xb_scr, acc_scr, key2b_scr, a2b_scr, *sub_scr, n_steps, nsub, gsub, cw):
    e = pl.program_id(1)
    tt = x_ref.shape[0]
    h_scrs, a_scrs = sub_scr[:nsub], sub_scr[nsub:]
    sub = gsub * PEER_NKEYS
    pk = 16
    grp = PEER_NKEYS // pk

    @pl.when(e == 0)
    def _():
        xb_scr[...] = x_ref[...].astype(BF16)
        acc_scr[...] = jnp.zeros(acc_scr.shape, F32)
        key2b_scr[...] = key2_ref[...].astype(BF16)
        a2b_scr[...] = a2_ref[...].astype(BF16)

    for s in range(nsub):
        h_scrs[s][...] = lax.dot_general(u_ref[s * sub:(s + 1) * sub, :], xb_scr[...], NT_DIMS,
                                         preferred_element_type=F32)
    for s in range(nsub):
        for gl in range(gsub):
            gi = s * gsub + gl
            rs = slice(gl * PEER_NKEYS, (gl + 1) * PEER_NKEYS)
            for cc in range(tt // cw):
                cs = slice(cc * cw, (cc + 1) * cw)
                wd = jnp.zeros((grp, pk, cw), BF16)
                for h in range(PEER_HEADS):
                    thr_b = jnp.broadcast_to(thr_ref[h, gi, :, cs], (pk, cw)).astype(BF16)
                    a1_b = jnp.broadcast_to(a1_ref[h, gi, :, cs], (pk, cw)).astype(BF16)
                    k2 = key2b_scr[h, :, cs].reshape(grp, pk, cw)
                    a2 = a2b_scr[h, :, cs].reshape(grp, pk, cw)
                    wd = wd + jnp.where(k2 < thr_b[None], a2, jnp.zeros_like(a2)) * a1_b[None]
                hv = h_scrs[s][rs, cs]
                gelu = 0.5 * hv * (1.0 + lax.erf(hv * F32(1.0 / math.sqrt(2.0))))
                a_scrs[s][rs, cs] = (gelu * wd.reshape(PEER_NKEYS, cw).astype(F32)).astype(BF16)
        acc_scr[...] += jnp.dot(vt_ref[0, :, s * sub:(s + 1) * sub], a_scrs[s][...],
                                preferred_element_type=F32)

    @pl.when(e == n_steps - 1)
    def _():
        y = jnp.transpose(acc_scr[...])
        o_ref[...] = _layer_norm(ALPHA * x_ref[...] + y, g_ref[...], b_ref[...])


def _peer_ffn(x, u_bf, vt_bf, thr, a1, key2, a2, g, b, *, tt=512, cw=128, gsub=2):
    t = x.shape[0]
    te = vt_bf.shape[2]
    tt = min(tt, t)
    cw = min(cw, tt)
    ng = te // PEER_NKEYS
    nsub = ng // gsub
    n_steps = PEER_N // te
    kern = functools.partial(_peer_ffn_kernel, n_steps=n_steps, nsub=nsub, gsub=gsub, cw=cw)
    return pl.pallas_call(
        kern, grid=(t // tt, n_steps),
        in_specs=[
            pl.BlockSpec((tt, D_MODEL), lambda i, e: (i, 0)),
            pl.BlockSpec((te, D_MODEL), lambda i, e: (e, 0)),
            pl.BlockSpec((1, D_MODEL, te), lambda i, e: (e, 0, 0)),
            pl.BlockSpec((PEER_HEADS, ng, 1, tt), lambda i, e: (0, e, 0, i)),
            pl.BlockSpec((PEER_HEADS, ng, 1, tt), lambda i, e: (0, e, 0, i)),
            pl.BlockSpec((PEER_HEADS, PEER_NKEYS, tt), lambda i, e: (0, 0, i)),
            pl.BlockSpec((PEER_HEADS, PEER_NKEYS, tt), lambda i, e: (0, 0, i)),
            pl.BlockSpec((1, D_MODEL), lambda i, e: (0, 0)),
            pl.BlockSpec((1, D_MODEL), lambda i, e: (0, 0)),
        ],
        out_specs=pl.BlockSpec((tt, D_MODEL), lambda i, e: (i, 0)),
        out_shape=jax.ShapeDtypeStruct((t, D_MODEL), F32),
        scratch_shapes=([pltpu.VMEM((tt, D_MODEL), BF16), pltpu.VMEM((D_MODEL, tt), F32)]
                        + [pltpu.VMEM((PEER_HEADS, PEER_NKEYS, tt), BF16)] * 2
                        + [pltpu.VMEM((gsub * PEER_NKEYS, tt), F32)] * nsub
                        + [pltpu.VMEM((gsub * PEER_NKEYS, tt), BF16)] * nsub),
        compiler_params=_cparams(("parallel", "arbitrary")),
        name="peer_ffn")(x, u_bf, vt_bf, thr, a1, key2, a2,
                         g.reshape(1, D_MODEL), b.reshape(1, D_MODEL))


def _peer_layer(x, wq_bf, sk, u_bf, vt_bf, g, b):
    s1, s2 = _peer_scores(x, wq_bf, sk)
    thr, a1, key2, a2 = _peer_topk(s1, s2)
    return _peer_ffn(x, u_bf, vt_bf, thr, a1, key2, a2, g, b)


def _ret_rot_tables(pos):
    half = RET_DK // 2
    inv = 1.0 / (RET_THETA ** (jnp.arange(half, dtype=F32) / half))
    ang = pos.astype(F32)[:, None] * inv[None, :]
    return jnp.cos(ang), jnp.sin(ang)


def _da_rot_tables(pos):
    half = DA_ROT // 2
    inv = 1.0 / (ROPE_THETA ** (jnp.arange(half, dtype=F32) / half))
    ang = pos.astype(F32)[:, None] * inv[None, :]
    cos, sin = jnp.cos(ang), jnp.sin(ang)
    n = pos.shape[0]
    pad = jnp.zeros((n, DA_DK - DA_ROT), F32)
    c64 = jnp.concatenate([cos, cos, pad + 1.0], axis=1)
    lo64 = jnp.concatenate([-sin, jnp.zeros((n, half), F32), pad], axis=1)
    hi64 = jnp.concatenate([jnp.zeros((n, half), F32), sin, pad], axis=1)
    rep = LANES // DA_DK
    return jnp.tile(c64, (1, rep)), jnp.tile(lo64, (1, rep)), jnp.tile(hi64, (1, rep))


def _ret_layer(x, r0, w_in_bf, w_out_bf, rot_tabs, g, b, *, batch, seq, true_len, tm):
    qk = _proj(x, w_in_bf[:, :2 * RET_QK], tm=tm, tn=RET_DK, out_dtypes=(F32,), rot="full",
               tabs=rot_tabs, scales=(1.0, RET_DK ** -0.5), n_scale_split=RET_HEADS)[0]
    vg = _proj(x, w_in_bf[:, 2 * RET_QK:], tm=tm, tn=512, out_dtypes=(F32,))[0]
    if seq != true_len:
        pad = ((0, 0), (0, seq - true_len), (0, 0))
        qk = jnp.pad(qk.reshape(batch, true_len, -1), pad).reshape(batch * seq, -1)
        vg = jnp.pad(vg.reshape(batch, true_len, -1), pad).reshape(batch * seq, -1)
    y, r_fin = _retention(qk, vg, r0, batch=batch, seq=seq, chunk=RET_CHUNK, true_len=min(true_len, RET_CHUNK))
    if seq != true_len:
        y = y.reshape(batch, seq, RET_V)[:, :true_len].reshape(batch * true_len, RET_V)
    return _out_ln(y, w_out_bf, x, g, b, tm=tm), r_fin


def _da_project(x, w_in_bf, rot_tabs, *, tm):
    q = _proj(x, w_in_bf[:, :DA_QK], tm=tm, tn=512, out_dtypes=(BF16,), rot="partial",
              tabs=rot_tabs, scales=(DA_DK ** -0.5, DA_DK ** -0.5), n_scale_split=0)[0]
    k, kb = _proj(x, w_in_bf[:, DA_QK:2 * DA_QK], tm=tm, tn=512, out_dtypes=(F32, BF16),
                  rot="partial", tabs=rot_tabs)
    v = _proj(x, w_in_bf[:, 2 * DA_QK:], tm=tm, tn=512, out_dtypes=(F32,))[0]
    return q, k, kb, v


def kernel(x_prompt, x_sample, state_ret_l0, cache_k_l1, cache_v_l1, state_ret_l2, cache_k_l3,
           cache_v_l3, page_table, ret_w_in, ret_w_out, da_w_in, da_w_out, da_lambda, da_subln_g,
           peer_w_q, peer_sub_keys, peer_u, peer_v, ln_g, ln_b):
    bp, sp, _ = x_prompt.shape
    bs, ts, _ = x_sample.shape
    xp = x_prompt.reshape(bp * sp, D_MODEL)
    xs = x_sample.reshape(bs * ts, D_MODEL)
    ret_states = [state_ret_l0, state_ret_l2]
    da_caches = [(cache_k_l1, cache_v_l1), (cache_k_l3, cache_v_l3)]

    pos_p = jnp.arange(sp)
    pos_s = jnp.tile(PAST_LEN + jnp.arange(ts), bs)
    ret_tab_p, ret_tab_s = _ret_rot_tables(pos_p), _ret_rot_tables(pos_s)
    da_tab_p, da_tab_s = _da_rot_tables(pos_p), _da_rot_tables(pos_s)
    tm_p, tm_s = 512, bs * ts
    te = 1024

    new_states = []
    for i in range(DEPTH):
        j = i // 2
        if i % 2 == 0:
            w_in = ret_w_in[j].astype(BF16)
            w_out = ret_w_out[j].astype(BF16)
            r0p = jnp.zeros((bp, RET_HEADS, RET_DK, RET_DV), F32)
            xp, rp = _ret_layer(xp, r0p, w_in, w_out, ret_tab_p, ln_g[i, 0], ln_b[i, 0],
                                batch=bp, seq=sp, true_len=sp, tm=tm_p)
            xs, rs = _ret_layer(xs, ret_states[j], w_in, w_out, ret_tab_s, ln_g[i, 0], ln_b[i, 0],
                                batch=bs, seq=RET_CHUNK, true_len=ts, tm=tm_s)
            new_states.append((rp, rs))
        else:
            w_in = da_w_in[j].astype(BF16)
            w_out = da_w_out[j].astype(BF16)
            ck, cv = da_caches[j]
            qp, kp, kpb, vp = _da_project(xp, w_in, da_tab_p, tm=tm_p)
            vtp = _proj_t(xp, w_in[:, 2 * DA_QK:].T, tm=tm_p, out_dtype=BF16)
            yp = _attn_prompt(qp, kpb, vtp, da_lambda[j], da_subln_g[j], batch=bp, seq=sp,
                              layer_idx=i)
            qs, ks, _, vs = _da_project(xs, w_in, da_tab_s, tm=tm_s)
            ys = _attn_sample(qs, ks, vs, ck, cv, page_table, da_lambda[j], da_subln_g[j],
                              batch=bs, n_new=ts, layer_idx=i)
            xp = _out_ln(yp, w_out, xp, ln_g[i, 0], ln_b[i, 0], tm=tm_p, y_transposed=True)
            xs = _out_ln(ys, w_out, xs, ln_g[i, 0], ln_b[i, 0], tm=tm_s)
            new_states.append((kp.reshape(bp, sp, DA_HEADS, 2, DA_DK),
                               vp.reshape(bp, sp, DA_HEADS, DA_DV),
                               ks.reshape(bs, ts, DA_HEADS, 2, DA_DK),
                               vs.reshape(bs, ts, DA_HEADS, DA_DV)))
        wq = peer_w_q[i].astype(BF16)
        u_bf = peer_u[i].astype(BF16)
        vt_bf = peer_v[i].astype(BF16).reshape(PEER_N // te, te, D_MODEL).transpose(0, 2, 1)
        sk = peer_sub_keys[i]
        xp = _peer_layer(xp, wq, sk, u_bf, vt_bf, ln_g[i, 1], ln_b[i, 1])
        xs = _peer_layer(xs, wq, sk, u_bf, vt_bf, ln_g[i, 1], ln_b[i, 1])

    (rp0, rs0), (kp1, vp1, ks1, vs1), (rp2, rs2), (kp3, vp3, ks3, vs3) = new_states
    return (xp.reshape(bp, sp, D_MODEL), xs.reshape(bs, ts, D_MODEL), rp0, rs0, kp1, vp1, ks1, vs1,
            rp2, rs2, kp3, vp3, ks3, vs3)
```

```python
import functools
import math

import jax
import jax.numpy as jnp
from jax import lax
from jax.experimental import pallas as pl
from jax.experimental.pallas import tpu as pltpu

F32 = jnp.float32
BF16 = jnp.bfloat16

D_MODEL = 1024
DEPTH = 4
PAST_LEN = 8192
PAGE_SIZE = 128

RET_HEADS = 4
RET_DK = 256
RET_DV = 512
RET_QK = RET_HEADS * RET_DK
RET_V = RET_HEADS * RET_DV
RET_CHUNK = 128
RET_THETA = 10000.0

DA_HEADS = 8
DA_DK = 64
DA_DV = 128
DA_QK = DA_HEADS * 2 * DA_DK
DA_V = DA_HEADS * DA_DV
DA_ROT = DA_DK // 4
ROPE_THETA = 500000.0

PEER_HEADS = 8
PEER_NKEYS = 128
PEER_N = PEER_NKEYS * PEER_NKEYS
PEER_DKEY = 256
PEER_TOPK = 16

LN_EPS = 1e-5
NORM_EPS = 1e-5
ALPHA = (2 * DEPTH) ** 0.25

LANES = 128
VMEM_LIMIT = 56 * 1024 * 1024

NT_DIMS = (((1,), (1,)), ((), ()))
TN_DIMS = (((0,), (0,)), ((), ()))


def _cparams(sem):
    return pltpu.CompilerParams(dimension_semantics=sem, vmem_limit_bytes=VMEM_LIMIT)


def _layer_norm(z, g, b):
    mu = jnp.mean(z, axis=-1, keepdims=True)
    d = z - mu
    var = jnp.mean(d * d, axis=-1, keepdims=True)
    return d * lax.rsqrt(var + LN_EPS) * g + b


def _proj_kernel(*refs, rot, n_tab, scales, n_scale_split):
    x_ref, w_ref = refs[0], refs[1]
    tabs = refs[2:2 + n_tab]
    outs = refs[2 + n_tab:]
    acc = jnp.dot(x_ref[...].astype(BF16), w_ref[...], preferred_element_type=F32)
    tn = acc.shape[1]
    if rot == "full":
        cos, sin = tabs[0][...], tabs[1][...]
        x1, x2 = acc[:, :LANES], acc[:, LANES:]
        acc = jnp.concatenate([x1 * cos - x2 * sin, x2 * cos + x1 * sin], axis=1)
    elif rot == "partial":
        c, s_lo, s_hi = tabs[0][...], tabs[1][...], tabs[2][...]
        blocks = []
        for cb in range(tn // LANES):
            blk = acc[:, cb * LANES:(cb + 1) * LANES]
            nxt = pltpu.roll(blk, LANES - DA_ROT // 2, 1)
            prv = pltpu.roll(blk, DA_ROT // 2, 1)
            blocks.append(blk * c + nxt * s_lo + prv * s_hi)
        acc = jnp.concatenate(blocks, axis=1) if len(blocks) > 1 else blocks[0]
    if scales is not None:
        j = pl.program_id(1)
        acc = acc * jnp.where(j < n_scale_split, F32(scales[0]), F32(scales[1]))
    for o in outs:
        o[...] = acc.astype(o.dtype)


def _proj(x, w, *, tm, tn, out_dtypes, rot=None, tabs=(), scales=None, n_scale_split=0):
    t, k = x.shape
    n = w.shape[1]
    tm = min(tm, t)
    grid = (t // tm, n // tn)
    in_specs = [pl.BlockSpec((tm, k), lambda i, j: (i, 0)),
                pl.BlockSpec((k, tn), lambda i, j: (0, j))]
    for tab in tabs:
        nrb = tab.shape[0] // tm
        in_specs.append(pl.BlockSpec((tm, LANES), lambda i, j, nrb=nrb: (i % nrb, 0)))
    out_shape = [jax.ShapeDtypeStruct((t, n), dt) for dt in out_dtypes]
    out_specs = [pl.BlockSpec((tm, tn), lambda i, j: (i, j)) for _ in out_dtypes]
    kern = functools.partial(_proj_kernel, rot=rot, n_tab=len(tabs), scales=scales,
                             n_scale_split=n_scale_split)
    return pl.pallas_call(kern, grid=grid, in_specs=in_specs, out_specs=out_specs,
                          out_shape=out_shape,
                          compiler_params=_cparams(("parallel", "arbitrary")),
                          name="proj")(x, w, *tabs)


def _proj_t_kernel(wt_ref, x_ref, o_ref):
    o_ref[...] = lax.dot_general(wt_ref[...], x_ref[...].astype(BF16), NT_DIMS,
                                 preferred_element_type=F32).astype(o_ref.dtype)


def _proj_t(x, wt, *, tm, out_dtype):
    t, k = x.shape
    n = wt.shape[0]
    tm = min(tm, t)
    return pl.pallas_call(
        _proj_t_kernel, grid=(t // tm,),
        in_specs=[pl.BlockSpec((n, k), lambda i: (0, 0)),
                  pl.BlockSpec((tm, k), lambda i: (i, 0))],
        out_specs=pl.BlockSpec((n, tm), lambda i: (0, i)),
        out_shape=jax.ShapeDtypeStruct((n, t), out_dtype),
        compiler_params=_cparams(("parallel",)),
        name="proj_t")(wt, x)


def _out_ln_kernel(y_ref, w_ref, x_ref, g_ref, b_ref, o_ref, *, y_transposed):
    dims = TN_DIMS if y_transposed else (((1,), (0,)), ((), ()))
    m = lax.dot_general(y_ref[...].astype(BF16), w_ref[...], dims, preferred_element_type=F32)
    o_ref[...] = _layer_norm(ALPHA * x_ref[...] + m, g_ref[...], b_ref[...])


def _out_ln(y, w, x, g, b, *, tm=512, y_transposed=False):
    k, n = w.shape
    t = x.shape[0]
    tm = min(tm, t)
    y_spec = (pl.BlockSpec((k, tm), lambda i: (0, i)) if y_transposed
              else pl.BlockSpec((tm, k), lambda i: (i, 0)))
    return pl.pallas_call(
        functools.partial(_out_ln_kernel, y_transposed=y_transposed), grid=(t // tm,),
        in_specs=[y_spec,
                  pl.BlockSpec((k, n), lambda i: (0, 0)),
                  pl.BlockSpec((tm, n), lambda i: (i, 0)),
                  pl.BlockSpec((1, n), lambda i: (0, 0)),
                  pl.BlockSpec((1, n), lambda i: (0, 0))],
        out_specs=pl.BlockSpec((tm, n), lambda i: (i, 0)),
        out_shape=jax.ShapeDtypeStruct((t, n), F32),
        compiler_params=_cparams(("parallel",)),
        name="out_ln")(y, w, x, g.reshape(1, n), b.reshape(1, n))


def _ret_kernel(q_ref, k_ref, v_ref, g_ref, r0_ref, dm_ref, qd_ref, kd_ref, gl_ref,
                y_ref, rout_ref, r_scr, *, nc):
    c = pl.program_id(2)

    @pl.when(c == 0)
    def _():
        r_scr[...] = r0_ref[0, 0]

    q = q_ref[0]
    k = k_ref[0]
    vb = v_ref[0].astype(BF16)
    qb = q.astype(BF16)
    qk = lax.dot_general(qb, k.astype(BF16), NT_DIMS, preferred_element_type=F32)
    inner = (qk * dm_ref[0]).astype(BF16)
    r = r_scr[...]
    o = jnp.dot(inner, vb, preferred_element_type=F32)
    o = o + jnp.dot(qb, r.astype(BF16), preferred_element_type=F32) * qd_ref[0]
    kdec = (k * kd_ref[0]).astype(BF16)
    r_new = gl_ref[0] * r + lax.dot_general(kdec, vb, TN_DIMS, preferred_element_type=F32)
    r_scr[...] = r_new

    mu = jnp.mean(o, axis=-1, keepdims=True)
    d = o - mu
    var = jnp.mean(d * d, axis=-1, keepdims=True)
    on = d * lax.rsqrt(var + NORM_EPS)
    gg = g_ref[0]
    y_ref[0] = (gg * jax.nn.sigmoid(gg) * on).astype(y_ref.dtype)

    @pl.when(c == nc - 1)
    def _():
        rout_ref[0, 0] = r_new


def _ret_decay_tables(chunk, true_len):
    lg = jnp.log(1.0 - 2.0 ** (-5.0 - jnp.arange(RET_HEADS, dtype=F32)))
    idx = jnp.arange(chunk, dtype=F32)
    diff = idx[:, None] - idx[None, :]
    dm = jnp.where(diff >= 0, jnp.exp(lg[:, None, None] * jnp.maximum(diff, 0.0)), 0.0)
    qd = jnp.exp(lg[:, None] * (idx[None, :] + 1.0))[:, :, None]
    kd = jnp.where(idx[None, :] < true_len,
                   jnp.exp(lg[:, None] * (true_len - 1.0 - idx[None, :])), 0.0)[:, :, None]
    gl = jnp.exp(lg * true_len)[:, None, None]
    return dm.astype(F32), qd.astype(F32), kd.astype(F32), gl.astype(F32)


def _retention(qk, vg, r0, *, batch, seq, chunk, true_len):
    nc = seq // chunk
    qk3 = qk.reshape(batch, seq, 2 * RET_QK)
    vg3 = vg.reshape(batch, seq, 2 * RET_V)
    dm, qd, kd, gl = _ret_decay_tables(chunk, true_len)
    kern = functools.partial(_ret_kernel, nc=nc)
    y, r_fin = pl.pallas_call(
        kern, grid=(batch, RET_HEADS, nc),
        in_specs=[
            pl.BlockSpec((1, chunk, RET_DK), lambda b, h, c: (b, c, h)),
            pl.BlockSpec((1, chunk, RET_DK), lambda b, h, c: (b, c, RET_HEADS + h)),
            pl.BlockSpec((1, chunk, RET_DV), lambda b, h, c: (b, c, h)),
            pl.BlockSpec((1, chunk, RET_DV), lambda b, h, c: (b, c, RET_HEADS + h)),
            pl.BlockSpec((1, 1, RET_DK, RET_DV), lambda b, h, c: (b, h, 0, 0)),
            pl.BlockSpec((1, chunk, chunk), lambda b, h, c: (h, 0, 0)),
            pl.BlockSpec((1, chunk, 1), lambda b, h, c: (h, 0, 0)),
            pl.BlockSpec((1, chunk, 1), lambda b, h, c: (h, 0, 0)),
            pl.BlockSpec((1, 1, 1), lambda b, h, c: (h, 0, 0)),
        ],
        out_specs=[
            pl.BlockSpec((1, chunk, RET_DV), lambda b, h, c: (b, c, h)),
            pl.BlockSpec((1, 1, RET_DK, RET_DV), lambda b, h, c: (b, h, 0, 0)),
        ],
        out_shape=[jax.ShapeDtypeStruct((batch, seq, RET_V), BF16),
                   jax.ShapeDtypeStruct((batch, RET_HEADS, RET_DK, RET_DV), F32)],
        scratch_shapes=[pltpu.VMEM((RET_DK, RET_DV), F32)],
        compiler_params=_cparams(("parallel", "parallel", "arbitrary")),
        name="retention")(qk3, qk3, vg3, vg3, r0, dm, qd, kd, gl)
    return y.reshape(batch * seq, RET_V), r_fin


def _diff_lambda(lam_ref, layer_idx):
    lv = lam_ref[...]
    lam_init = 0.8 - 0.6 * math.exp(-0.3 * layer_idx)
    t1 = jnp.sum(lv[0:1, :] * lv[1:2, :], axis=-1, keepdims=True)
    t2 = jnp.sum(lv[2:3, :] * lv[3:4, :], axis=-1, keepdims=True)
    return jnp.exp(t1) - jnp.exp(t2) + lam_init, lam_init


def _attn_kernel(qi_ref, kj_ref, q_ref, k_ref, vt_ref, lam_ref, sg_ref, o_ref,
                 qz_scr, m_scr, l_scr, acc_scr, *, tq, tk, kw, layer_idx):
    p = pl.program_id(2)
    qi = qi_ref[p]
    kj = kj_ref[p]
    last_kj = (qi * tq + tq - 1) // tk

    @pl.when(kj == 0)
    def _():
        qt = jnp.transpose(q_ref[0].astype(F32))
        feat = lax.broadcasted_iota(jnp.int32, qt.shape, 0)
        zero = jnp.zeros_like(qt)
        qz_scr[:, 0:tq] = jnp.where(feat < DA_DK, qt, zero).astype(BF16)
        qz_scr[:, tq:2 * tq] = jnp.where(feat >= DA_DK, qt, zero).astype(BF16)
        m_scr[...] = jnp.full(m_scr.shape, -jnp.inf, F32)
        l_scr[...] = jnp.zeros(l_scr.shape, F32)
        acc_scr[...] = jnp.zeros(acc_scr.shape, F32)

    def process(masked):
        m_run = m_scr[...]
        l_run = l_scr[...]
        acc = acc_scr[...]
        for sb in range(tk // kw):
            s = jnp.dot(k_ref[0, sb * kw:(sb + 1) * kw, :], qz_scr[...],
                        preferred_element_type=F32)
            if masked:
                row = lax.broadcasted_iota(jnp.int32, s.shape, 0)
                col = lax.broadcasted_iota(jnp.int32, s.shape, 1)
                kpos = kj * tk + sb * kw + row
                qpos = qi * tq + jnp.where(col >= tq, col - tq, col)
                s = jnp.where(kpos <= qpos, s, -jnp.inf)
            m_new = jnp.maximum(m_run, jnp.max(s, axis=0, keepdims=True))
            alpha = jnp.exp(m_run - m_new)
            pm = jnp.exp(s - m_new)
            l_run = alpha * l_run + jnp.sum(pm, axis=0, keepdims=True)
            acc = alpha * acc + jnp.dot(vt_ref[:, sb * kw:(sb + 1) * kw], pm.astype(BF16),
                                        preferred_element_type=F32)
            m_run = m_new
        m_scr[...] = m_run
        l_scr[...] = l_run
        acc_scr[...] = acc

    @pl.when(kj < last_kj)
    def _():
        process(False)

    @pl.when(kj == last_kj)
    def _():
        process(True)
        lam, lam_init = _diff_lambda(lam_ref, layer_idx)
        o = acc_scr[...] / l_scr[...]
        od = o[:, 0:tq] - lam * o[:, tq:2 * tq]
        od = od * lax.rsqrt(jnp.mean(od * od, axis=0, keepdims=True) + NORM_EPS)
        o_ref[...] = (od * sg_ref[...] * (1.0 - lam_init)).astype(o_ref.dtype)


def _attn_prompt(qb, kb, vt, lam_vec, subln_g, *, batch, seq, layer_idx, tq=512, tk=1024, kw=512):
    tq = min(tq, seq)
    tk = min(tk, seq)
    kw = min(kw, tk)
    nq, nk = seq // tq, seq // tk
    pairs = [(i, j) for i in range(nq) for j in range((i * tq + tq - 1) // tk + 1)]
    qi = jnp.asarray([pq for pq, _ in pairs], jnp.int32)
    kj = jnp.asarray([pk for _, pk in pairs], jnp.int32)
    q3 = qb.reshape(batch, seq, DA_QK)
    k3 = kb.reshape(batch, seq, DA_QK)
    kern = functools.partial(_attn_kernel, tq=tq, tk=tk, kw=kw, layer_idx=layer_idx)
    grid_spec = pltpu.PrefetchScalarGridSpec(
        num_scalar_prefetch=2, grid=(batch, DA_HEADS, len(pairs)),
        in_specs=[
            pl.BlockSpec((1, tq, LANES), lambda b, h, p, qi, kj: (b, qi[p], h)),
            pl.BlockSpec((1, tk, LANES), lambda b, h, p, qi, kj: (b, kj[p], h)),
            pl.BlockSpec((DA_DV, tk), lambda b, h, p, qi, kj: (h, b * nk + kj[p])),
            pl.BlockSpec((4, DA_DK), lambda b, h, p, qi, kj: (0, 0)),
            pl.BlockSpec((DA_DV, 1), lambda b, h, p, qi, kj: (0, 0)),
        ],
        out_specs=pl.BlockSpec((DA_DV, tq), lambda b, h, p, qi, kj: (h, b * nq + qi[p])),
        scratch_shapes=[pltpu.VMEM((LANES, 2 * tq), BF16),
                        pltpu.VMEM((1, 2 * tq), F32),
                        pltpu.VMEM((1, 2 * tq), F32),
                        pltpu.VMEM((DA_DV, 2 * tq), F32)])
    return pl.pallas_call(
        kern, grid_spec=grid_spec,
        out_shape=jax.ShapeDtypeStruct((DA_V, batch * seq), BF16),
        compiler_params=_cparams(("parallel", "parallel", "arbitrary")),
        name="attn_prompt")(qi, kj, q3, k3, vt, lam_vec, subln_g.reshape(DA_DV, 1))


def _attn_sample_kernel(pt_ref, q_ref, *refs, pp, n_steps, n_new, layer_idx):
    k_refs = refs[:pp]
    v_refs = refs[pp:2 * pp]
    kn_ref, vn_ref, lam_ref, sg_ref, o_ref, m_scr, l_scr, acc_scr = refs[2 * pp:]
    step = pl.program_id(1)
    rows = q_ref.shape[1]

    @pl.when(step == 0)
    def _():
        m_scr[...] = jnp.full(m_scr.shape, -jnp.inf, F32)
        l_scr[...] = jnp.zeros(l_scr.shape, F32)
        acc_scr[...] = jnp.zeros(acc_scr.shape, F32)

    row = lax.broadcasted_iota(jnp.int32, (rows, LANES), 0)
    lane = lax.broadcasted_iota(jnp.int32, (rows, LANES), 1)
    keep = ((row < n_new) == (lane < DA_DK)) & (row < 2 * n_new)

    def head_q(h):
        qh = q_ref[0][:, h * LANES:(h + 1) * LANES]
        return jnp.where(keep, qh, jnp.zeros_like(qh))

    for h in range(DA_HEADS):
        qz = head_q(h)
        cs = slice(h * LANES, (h + 1) * LANES)
        s_list = [lax.dot_general(qz, k_refs[u][0][:, cs].astype(BF16), NT_DIMS,
                                  preferred_element_type=F32) for u in range(pp)]
        mx = s_list[0].max(axis=-1, keepdims=True)
        for u in range(1, pp):
            mx = jnp.maximum(mx, s_list[u].max(axis=-1, keepdims=True))
        m_old = m_scr[h][:, 0:1]
        m_new = jnp.maximum(m_old, mx)
        alpha = jnp.exp(m_old - m_new)
        l_new = alpha * l_scr[h][:, 0:1]
        acc = alpha * acc_scr[h]
        for u in range(pp):
            pu = jnp.exp(s_list[u] - m_new)
            l_new = l_new + jnp.sum(pu, axis=-1, keepdims=True)
            acc = acc + jnp.dot(pu.astype(BF16), v_refs[u][0, :, h, :].astype(BF16),
                                preferred_element_type=F32)
        m_scr[h] = jnp.broadcast_to(m_new, (rows, LANES))
        l_scr[h] = jnp.broadcast_to(l_new, (rows, LANES))
        acc_scr[h] = acc

    @pl.when(step == n_steps - 1)
    def _():
        lam, lam_init = _diff_lambda(lam_ref, layer_idx)
        tok = jnp.where(row >= n_new, row - n_new, row)[:, 0:1]
        for h in range(DA_HEADS):
            qf = head_q(h).astype(F32)
            cs = slice(h * LANES, (h + 1) * LANES)
            kn = kn_ref[0][:, cs]
            vn = vn_ref[0][:, cs]
            s_new = []
            for j in range(n_new):
                sj = jnp.sum(qf * kn[j:j + 1, :], axis=-1, keepdims=True)
                s_new.append(jnp.where(tok >= j, sj, -jnp.inf))
            m_old = m_scr[h][:, 0:1]
            m_new = m_old
            for sj in s_new:
                m_new = jnp.maximum(m_new, sj)
            alpha = jnp.exp(m_old - m_new)
            l_new = alpha * l_scr[h][:, 0:1]
            acc = alpha * acc_scr[h]
            for j in range(n_new):
                pj = jnp.exp(s_new[j] - m_new)
                l_new = l_new + pj
                acc = acc + pj * vn[j:j + 1, :]
            o = acc / l_new
            od = o[0:n_new, :] - lam * o[n_new:2 * n_new, :]
            od = od * lax.rsqrt(jnp.mean(od * od, axis=-1, keepdims=True) + NORM_EPS)
            o_ref[0, :, cs] = od * sg_ref[...] * (1.0 - lam_init)


def _attn_sample(qb, k_new, v_new, cache_k, cache_v, page_table, lam_vec, subln_g, *,
                 batch, n_new, layer_idx, pp=4):
    n_pages = page_table.shape[1]
    n_phys = cache_k.shape[0]
    n_steps = n_pages // pp
    q3 = qb.reshape(batch, n_new, DA_QK)
    rows = 16
    q3 = jnp.concatenate([q3, q3, jnp.zeros((batch, rows - 2 * n_new, DA_QK), q3.dtype)], axis=1)
    kc = cache_k.reshape(n_phys, PAGE_SIZE, DA_QK)
    vc = cache_v
    kn = k_new.reshape(batch, n_new, DA_QK)
    vn = v_new.reshape(batch, n_new, DA_V)

    def page_spec(u):
        return pl.BlockSpec((1, PAGE_SIZE, DA_QK), lambda b, s, pt, u=u: (pt[b, s * pp + u], 0, 0))

    def vpage_spec(u):
        return pl.BlockSpec((1, PAGE_SIZE, DA_HEADS, DA_DV),
                            lambda b, s, pt, u=u: (pt[b, s * pp + u], 0, 0, 0))

    in_specs = ([pl.BlockSpec((1, rows, DA_QK), lambda b, s, pt: (b, 0, 0))]
                + [page_spec(u) for u in range(pp)] + [vpage_spec(u) for u in range(pp)]
                + [pl.BlockSpec((1, n_new, DA_QK), lambda b, s, pt: (b, 0, 0)),
                   pl.BlockSpec((1, n_new, DA_V), lambda b, s, pt: (b, 0, 0)),
                   pl.BlockSpec((4, DA_DK), lambda b, s, pt: (0, 0)),
                   pl.BlockSpec((1, DA_DV), lambda b, s, pt: (0, 0))])
    grid_spec = pltpu.PrefetchScalarGridSpec(
        num_scalar_prefetch=1, grid=(batch, n_steps), in_specs=in_specs,
        out_specs=pl.BlockSpec((1, n_new, DA_V), lambda b, s, pt: (b, 0, 0)),
        scratch_shapes=[pltpu.VMEM((DA_HEADS, rows, LANES), F32)] * 3)
    kern = functools.partial(_attn_sample_kernel, pp=pp, n_steps=n_steps, n_new=n_new,
                             layer_idx=layer_idx)
    y = pl.pallas_call(
        kern, grid_spec=grid_spec,
        out_shape=jax.ShapeDtypeStruct((batch, n_new, DA_V), F32),
        compiler_params=_cparams(("parallel", "arbitrary")),
        name="attn_sample")(page_table, q3, *([kc] * pp), *([vc] * pp), kn, vn, lam_vec,
                            subln_g.reshape(1, DA_DV))
    return y.reshape(batch * n_new, DA_V)


def _peer_scores_kernel(x_ref, wq_ref, sk_ref, s1_ref, s2_ref):
    q = jnp.dot(x_ref[...].astype(BF16), wq_ref[...], preferred_element_type=F32)
    for h in range(PEER_HEADS):
        for c in range(2):
            col = (h * 2 + c) * LANES
            st = lax.dot_general(sk_ref[c], q[:, col:col + LANES], NT_DIMS,
                                 precision=lax.Precision.HIGHEST, preferred_element_type=F32)
            (s1_ref if c == 0 else s2_ref)[h] = st


def _peer_scores(x, wq, sk, *, tm=512):
    t = x.shape[0]
    tm = min(tm, t)
    shp = jax.ShapeDtypeStruct((PEER_HEADS, PEER_NKEYS, t), F32)
    spec = pl.BlockSpec((PEER_HEADS, PEER_NKEYS, tm), lambda i: (0, 0, i))
    return pl.pallas_call(
        _peer_scores_kernel, grid=(t // tm,),
        in_specs=[pl.BlockSpec((tm, D_MODEL), lambda i: (i, 0)),
                  pl.BlockSpec((D_MODEL, PEER_HEADS * PEER_DKEY), lambda i: (0, 0)),
                  pl.BlockSpec((2, PEER_NKEYS, PEER_DKEY // 2), lambda i: (0, 0, 0))],
        out_specs=[spec, spec], out_shape=[shp, shp],
        compiler_params=_cparams(("parallel",)),
        name="peer_scores")(x, wq, sk)


def _peer_topk_kernel(s1_ref, s2_ref, thr_ref, a1_ref, key2_ref, a2_ref, rk1_scr):
    shp = s1_ref.shape[2:]
    neg = jnp.full(shp, -jnp.inf, F32)
    k = PEER_TOPK
    unranked = F32(k)

    def rank(s_ref, rk_ref):
        def init(n, carry):
            rk_ref[0, n] = jnp.full(shp, unranked, F32)
            return carry
        lax.fori_loop(0, PEER_NKEYS, init, 0)
        vals = []
        none = jnp.full(shp, -1.0, F32)
        win = none
        nch = 8
        for it in range(k + 1):
            def body(i, carry, it=it, win=win):
                ms, mis = carry
                new_ms, new_mis = [], []
                for p in range(nch):
                    n = i * nch + p
                    nf = F32(n)
                    r = rk_ref[0, n]
                    if it > 0:
                        r = jnp.where(win == nf, F32(it - 1), r)
                        rk_ref[0, n] = r
                    if it < k:
                        w = s_ref[0, n]
                        better = (r > k - 0.5) & (w > ms[p])
                        new_ms.append(jnp.where(better, w, ms[p]))
                        new_mis.append(jnp.where(better, nf, mis[p]))
                return (tuple(new_ms), tuple(new_mis)) if it < k else carry
            carry = ((neg,) * nch, (none,) * nch)
            for i in range(PEER_NKEYS // nch):
                carry = body(i, carry)
            ms, mis = carry
            if it < k:
                m, mi = ms[0], mis[0]
                for p in range(1, nch):
                    better = (ms[p] > m) | ((ms[p] == m) & (mis[p] < mi))
                    m = jnp.where(better, ms[p], m)
                    mi = jnp.where(better, mis[p], mi)
                vals.append(m)
                win = mi
        return vals

    a = rank(s1_ref, rk1_scr)
    b = rank(s2_ref, key2_ref)

    pairs = [(i, j) for i in range(k) for j in range(k) if (i + 1) * (j + 1) <= k]
    cv = [a[i] + b[j] for i, j in pairs]
    sel = [jnp.zeros(shp, F32) for _ in pairs]
    c0 = None
    z = jnp.zeros(shp, F32)
    for it in range(k):
        m = cv[0]
        for c in cv[1:]:
            m = jnp.maximum(m, c)
        if it == 0:
            c0 = m
        z = z + jnp.exp(m - c0)
        found = jnp.zeros(shp, F32)
        for idx in range(len(pairs)):
            hit = (cv[idx] == m) & (found < 0.5)
            sel[idx] = jnp.where(hit, 1.0, sel[idx])
            cv[idx] = jnp.where(hit, neg, cv[idx])
            found = jnp.where(hit, 1.0, found)
    cnt = [jnp.zeros(shp, F32) for _ in range(k)]
    for idx, (i, j) in enumerate(pairs):
        cnt[i] = cnt[i] + sel[idx]
    inv_z = 1.0 / z

    def emit(n, carry):
        r1 = rk1_scr[0, n]
        taken = jnp.zeros(shp, F32)
        for i in range(k):
            taken = jnp.where(r1 == F32(i), cnt[i], taken)
        thr_ref[0, n] = taken - 0.5
        a1_ref[0, n] = jnp.exp(s1_ref[0, n] - a[0]) * inv_z
        a2_ref[0, n] = jnp.exp(s2_ref[0, n] - b[0])
        return carry
    lax.fori_loop(0, PEER_NKEYS, emit, 0, unroll=4)


def _peer_topk(s1, s2):
    t = s1.shape[2]
    nt = t // LANES
    ts = min(8, nt)
    s1v = s1.reshape(PEER_HEADS, PEER_NKEYS, nt, LANES)
    s2v = s2.reshape(PEER_HEADS, PEER_NKEYS, nt, LANES)
    spec = pl.BlockSpec((1, PEER_NKEYS, ts, LANES), lambda h, i: (h, 0, i, 0))
    shp = jax.ShapeDtypeStruct(s1v.shape, F32)
    thr, a1, key2, a2 = pl.pallas_call(
        _peer_topk_kernel, grid=(PEER_HEADS, nt // ts),
        in_specs=[spec, spec], out_specs=[spec] * 4, out_shape=[shp] * 4,
        scratch_shapes=[pltpu.VMEM((1, PEER_NKEYS, ts, LANES), F32)],
        compiler_params=_cparams(("parallel", "parallel")),
        name="peer_topk")(s1v, s2v)
    flat = (PEER_HEADS, PEER_NKEYS, t)
    return thr.reshape(flat), a1.reshape(flat), key2.reshape(flat), a2.reshape(flat)


def _peer_ffn_kernel(x_ref, u_ref, vt_ref, thr_ref, a1_ref, key2_ref, a2_ref, g_ref, b_ref,
                     o_ref, xb_scr, acc_scr, *sub_scr, n_steps, nsub, gsub, cw):
    e = pl.program_id(1)
    tt = x_ref.shape[0]
    h_scrs, a_scrs = sub_scr[:nsub], sub_scr[nsub:]
    sub = gsub * PEER_NKEYS
    pk = 16
    grp = PEER_NKEYS // pk

    @pl.when(e == 0)
    def _():
        xb_scr[...] = jnp.transpose(x_ref[...]).astype(BF16)
        acc_scr[...] = jnp.zeros(acc_scr.shape, F32)

    for s in range(nsub):
        h_scrs[s][...] = jnp.dot(u_ref[s * sub:(s + 1) * sub, :], xb_scr[...],
                                 preferred_element_type=F32)
    for s in range(nsub):
        for gl in range(gsub):
            gi = s * gsub + gl
            rs = slice(gl * PEER_NKEYS, (gl + 1) * PEER_NKEYS)
            for cc in range(tt // cw):
                cs = slice(cc * cw, (cc + 1) * cw)
                wd = jnp.zeros((grp, pk, cw), BF16)
                for h in range(PEER_HEADS):
                    thr_b = jnp.broadcast_to(thr_ref[0, 0, h, gi:gi + 1, cs], (pk, cw)).astype(BF16)
                    a1_b = jnp.broadcast_to(a1_ref[0, 0, h, gi:gi + 1, cs], (pk, cw)).astype(BF16)
                    k2 = key2_ref[h, :, cs].reshape(grp, pk, cw)
                    a2 = a2_ref[h, :, cs].reshape(grp, pk, cw)
                    wd = wd + jnp.where(k2 < thr_b[None], a2, jnp.zeros_like(a2)) * a1_b[None]
                hv = h_scrs[s][rs, cs]
                gelu = 0.5 * hv * (1.0 + lax.erf(hv * F32(1.0 / math.sqrt(2.0))))
                a_scrs[s][rs, cs] = (gelu * wd.reshape(PEER_NKEYS, cw).astype(F32)).astype(BF16)
        acc_scr[...] += jnp.dot(vt_ref[0, :, s * sub:(s + 1) * sub], a_scrs[s][...],
                                preferred_element_type=F32)

    @pl.when(e == n_steps - 1)
    def _():
        y = jnp.transpose(acc_scr[...])
        o_ref[...] = _layer_norm(ALPHA * x_ref[...] + y, g_ref[...], b_ref[...])


def _peer_ffn(x, u_bf, vt_bf, thr, a1, key2, a2, g, b, *, tt=512, cw=128, gsub=2):
    t = x.shape[0]
    key2 = key2.astype(BF16)
    a2 = a2.astype(BF16)
    te = vt_bf.shape[2]
    tt = min(tt, t)
    cw = min(cw, tt)
    ng = te // PEER_NKEYS
    nsub = ng // gsub
    n_steps = PEER_N // te

    def per_step(a):
        a = a.reshape(PEER_HEADS, n_steps, ng, t // tt, tt)
        return a.transpose(3, 1, 0, 2, 4)

    step_spec = pl.BlockSpec((1, 1, PEER_HEADS, ng, tt), lambda i, e: (i, e, 0, 0, 0))
    kern = functools.partial(_peer_ffn_kernel, n_steps=n_steps, nsub=nsub, gsub=gsub, cw=cw)
    return pl.pallas_call(
        kern, grid=(t // tt, n_steps),
        in_specs=[
            pl.BlockSpec((tt, D_MODEL), lambda i, e: (i, 0)),
            pl.BlockSpec((te, D_MODEL), lambda i, e: (e, 0)),
            pl.BlockSpec((1, D_MODEL, te), lambda i, e: (e, 0, 0)),
            step_spec,
            step_spec,
            pl.BlockSpec((PEER_HEADS, PEER_NKEYS, tt), lambda i, e: (0, 0, i)),
            pl.BlockSpec((PEER_HEADS, PEER_NKEYS, tt), lambda i, e: (0, 0, i)),
            pl.BlockSpec((1, D_MODEL), lambda i, e: (0, 0)),
            pl.BlockSpec((1, D_MODEL), lambda i, e: (0, 0)),
        ],
        out_specs=pl.BlockSpec((tt, D_MODEL), lambda i, e: (i, 0)),
        out_shape=jax.ShapeDtypeStruct((t, D_MODEL), F32),
        scratch_shapes=([pltpu.VMEM((D_MODEL, tt), BF16), pltpu.VMEM((D_MODEL, tt), F32)]
                        + [pltpu.VMEM((gsub * PEER_NKEYS, tt), F32)] * nsub
                        + [pltpu.VMEM((gsub * PEER_NKEYS, tt), BF16)] * nsub),
        compiler_params=_cparams(("parallel", "arbitrary")),
        name="peer_ffn")(x, u_bf, vt_bf, per_step(thr), per_step(a1), key2, a2,
                         g.reshape(1, D_MODEL), b.reshape(1, D_MODEL))


def _peer_layer(x, wq_bf, sk, u_bf, vt_bf, g, b):
    s1, s2 = _peer_scores(x, wq_bf, sk)
    thr, a1, key2, a2 = _peer_topk(s1, s2)
    return _peer_ffn(x, u_bf, vt_bf, thr, a1, key2, a2, g, b)


def _ret_rot_tables(pos):
    half = RET_DK // 2
    inv = 1.0 / (RET_THETA ** (jnp.arange(half, dtype=F32) / half))
    ang = pos.astype(F32)[:, None] * inv[None, :]
    return jnp.cos(ang), jnp.sin(ang)


def _da_rot_tables(pos):
    half = DA_ROT // 2
    inv = 1.0 / (ROPE_THETA ** (jnp.arange(half, dtype=F32) / half))
    ang = pos.astype(F32)[:, None] * inv[None, :]
    cos, sin = jnp.cos(ang), jnp.sin(ang)
    n = pos.shape[0]
    pad = jnp.zeros((n, DA_DK - DA_ROT), F32)
    c64 = jnp.concatenate([cos, cos, pad + 1.0], axis=1)
    lo64 = jnp.concatenate([-sin, jnp.zeros((n, half), F32), pad], axis=1)
    hi64 = jnp.concatenate([jnp.zeros((n, half), F32), sin, pad], axis=1)
    rep = LANES // DA_DK
    return jnp.tile(c64, (1, rep)), jnp.tile(lo64, (1, rep)), jnp.tile(hi64, (1, rep))


def _ret_layer(x, r0, w_in_bf, w_out_bf, rot_tabs, g, b, *, batch, seq, true_len, tm):
    qk = _proj(x, w_in_bf[:, :2 * RET_QK], tm=tm, tn=RET_DK, out_dtypes=(F32,), rot="full",
               tabs=rot_tabs, scales=(1.0, RET_DK ** -0.5), n_scale_split=RET_HEADS)[0]
    vg = _proj(x, w_in_bf[:, 2 * RET_QK:], tm=tm, tn=512, out_dtypes=(F32,))[0]
    if seq != true_len:
        pad = ((0, 0), (0, seq - true_len), (0, 0))
        qk = jnp.pad(qk.reshape(batch, true_len, -1), pad).reshape(batch * seq, -1)
        vg = jnp.pad(vg.reshape(batch, true_len, -1), pad).reshape(batch * seq, -1)
    y, r_fin = _retention(qk, vg, r0, batch=batch, seq=seq, chunk=RET_CHUNK, true_len=min(true_len, RET_CHUNK))
    if seq != true_len:
        y = y.reshape(batch, seq, RET_V)[:, :true_len].reshape(batch * true_len, RET_V)
    return _out_ln(y, w_out_bf, x, g, b, tm=tm), r_fin


def _da_project(x, w_in_bf, rot_tabs, *, tm):
    q = _proj(x, w_in_bf[:, :DA_QK], tm=tm, tn=512, out_dtypes=(BF16,), rot="partial",
              tabs=rot_tabs, scales=(DA_DK ** -0.5, DA_DK ** -0.5), n_scale_split=0)[0]
    k, kb = _proj(x, w_in_bf[:, DA_QK:2 * DA_QK], tm=tm, tn=512, out_dtypes=(F32, BF16),
                  rot="partial", tabs=rot_tabs)
    v = _proj(x, w_in_bf[:, 2 * DA_QK:], tm=tm, tn=512, out_dtypes=(F32,))[0]
    return q, k, kb, v


def kernel(x_prompt, x_sample, state_ret_l0, cache_k_l1, cache_v_l1, state_ret_l2, cache_k_l3,
           cache_v_l3, page_table, ret_w_in, ret_w_out, da_w_in, da_w_out, da_lambda, da_subln_g,
           peer_w_q, peer_sub_keys, peer_u, peer_v, ln_g, ln_b):
    bp, sp, _ = x_prompt.shape
    bs, ts, _ = x_sample.shape
    xp = x_prompt.reshape(bp * sp, D_MODEL)
    xs = x_sample.reshape(bs * ts, D_MODEL)
    ret_states = [state_ret_l0, state_ret_l2]
    da_caches = [(cache_k_l1, cache_v_l1), (cache_k_l3, cache_v_l3)]

    pos_p = jnp.arange(sp)
    pos_s = jnp.tile(PAST_LEN + jnp.arange(ts), bs)
    ret_tab_p, ret_tab_s = _ret_rot_tables(pos_p), _ret_rot_tables(pos_s)
    da_tab_p, da_tab_s = _da_rot_tables(pos_p), _da_rot_tables(pos_s)
    tm_p, tm_s = 512, bs * ts
    te = 1024

    new_states = []
    for i in range(DEPTH):
        j = i // 2
        if i % 2 == 0:
            w_in = ret_w_in[j].astype(BF16)
            w_out = ret_w_out[j].astype(BF16)
            r0p = jnp.zeros((bp, RET_HEADS, RET_DK, RET_DV), F32)
            xp, rp = _ret_layer(xp, r0p, w_in, w_out, ret_tab_p, ln_g[i, 0], ln_b[i, 0],
                                batch=bp, seq=sp, true_len=sp, tm=tm_p)
            xs, rs = _ret_layer(xs, ret_states[j], w_in, w_out, ret_tab_s, ln_g[i, 0], ln_b[i, 0],
                                batch=bs, seq=RET_CHUNK, true_len=ts, tm=tm_s)
            new_states.append((rp, rs))
        else:
            w_in = da_w_in[j].astype(BF16)
            w_out = da_w_out[j].astype(BF16)
            ck, cv = da_caches[j]
            qp, kp, kpb, vp = _da_project(xp, w_in, da_tab_p, tm=tm_p)
            vtp = _proj_t(xp, w_in[:, 2 * DA_QK:].T, tm=tm_p, out_dtype=BF16)
            yp = _attn_prompt(qp, kpb, vtp, da_lambda[j], da_subln_g[j], batch=bp, seq=sp,
                              layer_idx=i)
            qs, ks, _, vs = _da_project(xs, w_in, da_tab_s, tm=tm_s)
            ys = _attn_sample(qs, ks, vs, ck, cv, page_table, da_lambda[j], da_subln_g[j],
                              batch=bs, n_new=ts, layer_idx=i)
            xp = _out_ln(yp, w_out, xp, ln_g[i, 0], ln_b[i, 0], tm=tm_p, y_transposed=True)
            xs = _out_ln(ys, w_out, xs, ln_g[i, 0], ln_b[i, 0], tm=tm_s)
            new_states.append((kp.reshape(bp, sp, DA_HEADS, 2, DA_DK),
                               vp.reshape(bp, sp, DA_HEADS, DA_DV),
                               ks.reshape(bs, ts, DA_HEADS, 2, DA_DK),
                               vs.reshape(bs, ts, DA_HEADS, DA_DV)))
        wq = peer_w_q[i].astype(BF16)
        u_bf = peer_u[i].astype(BF16)
        vt_bf = peer_v[i].astype(BF16).reshape(PEER_N // te, te, D_MODEL).transpose(0, 2, 1)
        sk = peer_sub_keys[i]
        xp = _peer_layer(xp, wq, sk, u_bf, vt_bf, ln_g[i, 1], ln_b[i, 1])
        xs = _peer_layer(xs, wq, sk, u_bf, vt_bf, ln_g[i, 1], ln_b[i, 1])

    (rp0, rs0), (kp1, vp1, ks1, vs1), (rp2, rs2), (kp3, vp3, ks3, vs3) = new_states
    return (xp.reshape(bp, sp, D_MODEL), xs.reshape(bs, ts, D_MODEL), rp0, rs0, kp1, vp1, ks1, vs1,
            rp2, rs2, kp3, vp3, ks3, vs3)
```

```python
import functools
import math

import jax
import jax.numpy as jnp
from jax import lax
from jax.experimental import pallas as pl
from jax.experimental.pallas import tpu as pltpu

F32 = jnp.float32
BF16 = jnp.bfloat16

D_MODEL = 1024
DEPTH = 4
PAST_LEN = 8192
PAGE_SIZE = 128

RET_HEADS = 4
RET_DK = 256
RET_DV = 512
RET_QK = RET_HEADS * RET_DK
RET_V = RET_HEADS * RET_DV
RET_CHUNK = 128
RET_THETA = 10000.0

DA_HEADS = 8
DA_DK = 64
DA_DV = 128
DA_QK = DA_HEADS * 2 * DA_DK
DA_V = DA_HEADS * DA_DV
DA_ROT = DA_DK // 4
ROPE_THETA = 500000.0

PEER_HEADS = 8
PEER_NKEYS = 128
PEER_N = PEER_NKEYS * PEER_NKEYS
PEER_DKEY = 256
PEER_TOPK = 16

LN_EPS = 1e-5
NORM_EPS = 1e-5
ALPHA = (2 * DEPTH) ** 0.25

LANES = 128
VMEM_LIMIT = 56 * 1024 * 1024

NT_DIMS = (((1,), (1,)), ((), ()))
TN_DIMS = (((0,), (0,)), ((), ()))


def _cparams(sem):
    return pltpu.CompilerParams(dimension_semantics=sem, vmem_limit_bytes=VMEM_LIMIT)


def _layer_norm(z, g, b):
    mu = jnp.mean(z, axis=-1, keepdims=True)
    d = z - mu
    var = jnp.mean(d * d, axis=-1, keepdims=True)
    return d * lax.rsqrt(var + LN_EPS) * g + b


def _proj_kernel(*refs, rot, n_tab, scales, n_scale_split):
    x_ref, w_ref = refs[0], refs[1]
    tabs = refs[2:2 + n_tab]
    outs = refs[2 + n_tab:]
    acc = jnp.dot(x_ref[...].astype(BF16), w_ref[...], preferred_element_type=F32)
    tn = acc.shape[1]
    if rot == "full":
        cos, sin = tabs[0][...], tabs[1][...]
        x1, x2 = acc[:, :LANES], acc[:, LANES:]
        acc = jnp.concatenate([x1 * cos - x2 * sin, x2 * cos + x1 * sin], axis=1)
    elif rot == "partial":
        c, s_lo, s_hi = tabs[0][...], tabs[1][...], tabs[2][...]
        blocks = []
        for cb in range(tn // LANES):
            blk = acc[:, cb * LANES:(cb + 1) * LANES]
            nxt = pltpu.roll(blk, LANES - DA_ROT // 2, 1)
            prv = pltpu.roll(blk, DA_ROT // 2, 1)
            blocks.append(blk * c + nxt * s_lo + prv * s_hi)
        acc = jnp.concatenate(blocks, axis=1) if len(blocks) > 1 else blocks[0]
    if scales is not None:
        j = pl.program_id(1)
        acc = acc * jnp.where(j < n_scale_split, F32(scales[0]), F32(scales[1]))
    for o in outs:
        o[...] = acc.astype(o.dtype)


def _proj(x, w, *, tm, tn, out_dtypes, rot=None, tabs=(), scales=None, n_scale_split=0):
    t, k = x.shape
    n = w.shape[1]
    tm = min(tm, t)
    grid = (t // tm, n // tn)
    in_specs = [pl.BlockSpec((tm, k), lambda i, j: (i, 0)),
                pl.BlockSpec((k, tn), lambda i, j: (0, j))]
    for tab in tabs:
        nrb = tab.shape[0] // tm
        in_specs.append(pl.BlockSpec((tm, LANES), lambda i, j, nrb=nrb: (i % nrb, 0)))
    out_shape = [jax.ShapeDtypeStruct((t, n), dt) for dt in out_dtypes]
    out_specs = [pl.BlockSpec((tm, tn), lambda i, j: (i, j)) for _ in out_dtypes]
    kern = functools.partial(_proj_kernel, rot=rot, n_tab=len(tabs), scales=scales,
                             n_scale_split=n_scale_split)
    return pl.pallas_call(kern, grid=grid, in_specs=in_specs, out_specs=out_specs,
                          out_shape=out_shape,
                          compiler_params=_cparams(("parallel", "arbitrary")),
                          name="proj")(x, w, *tabs)


def _proj_t_kernel(wt_ref, x_ref, o_ref):
    o_ref[...] = lax.dot_general(wt_ref[...], x_ref[...].astype(BF16), NT_DIMS,
                                 preferred_element_type=F32).astype(o_ref.dtype)


def _proj_t(x, wt, *, tm, out_dtype):
    t, k = x.shape
    n = wt.shape[0]
    tm = min(tm, t)
    return pl.pallas_call(
        _proj_t_kernel, grid=(t // tm,),
        in_specs=[pl.BlockSpec((n, k), lambda i: (0, 0)),
                  pl.BlockSpec((tm, k), lambda i: (i, 0))],
        out_specs=pl.BlockSpec((n, tm), lambda i: (0, i)),
        out_shape=jax.ShapeDtypeStruct((n, t), out_dtype),
        compiler_params=_cparams(("parallel",)),
        name="proj_t")(wt, x)


def _out_ln_kernel(y_ref, w_ref, x_ref, g_ref, b_ref, o_ref, *, y_transposed):
    dims = TN_DIMS if y_transposed else (((1,), (0,)), ((), ()))
    m = lax.dot_general(y_ref[...].astype(BF16), w_ref[...], dims, preferred_element_type=F32)
    o_ref[...] = _layer_norm(ALPHA * x_ref[...] + m, g_ref[...], b_ref[...])


def _out_ln(y, w, x, g, b, *, tm=512, y_transposed=False):
    k, n = w.shape
    t = x.shape[0]
    tm = min(tm, t)
    y_spec = (pl.BlockSpec((k, tm), lambda i: (0, i)) if y_transposed
              else pl.BlockSpec((tm, k), lambda i: (i, 0)))
    return pl.pallas_call(
        functools.partial(_out_ln_kernel, y_transposed=y_transposed), grid=(t // tm,),
        in_specs=[y_spec,
                  pl.BlockSpec((k, n), lambda i: (0, 0)),
                  pl.BlockSpec((tm, n), lambda i: (i, 0)),
                  pl.BlockSpec((1, n), lambda i: (0, 0)),
                  pl.BlockSpec((1, n), lambda i: (0, 0))],
        out_specs=pl.BlockSpec((tm, n), lambda i: (i, 0)),
        out_shape=jax.ShapeDtypeStruct((t, n), F32),
        compiler_params=_cparams(("parallel",)),
        name="out_ln")(y, w, x, g.reshape(1, n), b.reshape(1, n))


def _ret_kernel(q_ref, k_ref, v_ref, g_ref, r0_ref, dm_ref, qd_ref, kd_ref, gl_ref,
                y_ref, rout_ref, r_scr, *, nc):
    c = pl.program_id(2)

    @pl.when(c == 0)
    def _():
        r_scr[...] = r0_ref[0, 0]

    q = q_ref[0]
    k = k_ref[0]
    vb = v_ref[0].astype(BF16)
    qb = q.astype(BF16)
    qk = lax.dot_general(qb, k.astype(BF16), NT_DIMS, preferred_element_type=F32)
    inner = (qk * dm_ref[0]).astype(BF16)
    r = r_scr[...]
    o = jnp.dot(inner, vb, preferred_element_type=F32)
    o = o + jnp.dot(qb, r.astype(BF16), preferred_element_type=F32) * qd_ref[0]
    kdec = (k * kd_ref[0]).astype(BF16)
    r_new = gl_ref[0] * r + lax.dot_general(kdec, vb, TN_DIMS, preferred_element_type=F32)
    r_scr[...] = r_new

    mu = jnp.mean(o, axis=-1, keepdims=True)
    d = o - mu
    var = jnp.mean(d * d, axis=-1, keepdims=True)
    on = d * lax.rsqrt(var + NORM_EPS)
    gg = g_ref[0]
    y_ref[0] = (gg * jax.nn.sigmoid(gg) * on).astype(y_ref.dtype)

    @pl.when(c == nc - 1)
    def _():
        rout_ref[0, 0] = r_new


def _ret_decay_tables(chunk, true_len):
    lg = jnp.log(1.0 - 2.0 ** (-5.0 - jnp.arange(RET_HEADS, dtype=F32)))
    idx = jnp.arange(chunk, dtype=F32)
    diff = idx[:, None] - idx[None, :]
    dm = jnp.where(diff >= 0, jnp.exp(lg[:, None, None] * jnp.maximum(diff, 0.0)), 0.0)
    qd = jnp.exp(lg[:, None] * (idx[None, :] + 1.0))[:, :, None]
    kd = jnp.where(idx[None, :] < true_len,
                   jnp.exp(lg[:, None] * (true_len - 1.0 - idx[None, :])), 0.0)[:, :, None]
    gl = jnp.exp(lg * true_len)[:, None, None]
    return dm.astype(F32), qd.astype(F32), kd.astype(F32), gl.astype(F32)


def _retention(qk, vg, r0, *, batch, seq, chunk, true_len):
    nc = seq // chunk
    qk3 = qk.reshape(batch, seq, 2 * RET_QK)
    vg3 = vg.reshape(batch, seq, 2 * RET_V)
    dm, qd, kd, gl = _ret_decay_tables(chunk, true_len)
    kern = functools.partial(_ret_kernel, nc=nc)
    y, r_fin = pl.pallas_call(
        kern, grid=(batch, RET_HEADS, nc),
        in_specs=[
            pl.BlockSpec((1, chunk, RET_DK), lambda b, h, c: (b, c, h)),
            pl.BlockSpec((1, chunk, RET_DK), lambda b, h, c: (b, c, RET_HEADS + h)),
            pl.BlockSpec((1, chunk, RET_DV), lambda b, h, c: (b, c, h)),
            pl.BlockSpec((1, chunk, RET_DV), lambda b, h, c: (b, c, RET_HEADS + h)),
            pl.BlockSpec((1, 1, RET_DK, RET_DV), lambda b, h, c: (b, h, 0, 0)),
            pl.BlockSpec((1, chunk, chunk), lambda b, h, c: (h, 0, 0)),
            pl.BlockSpec((1, chunk, 1), lambda b, h, c: (h, 0, 0)),
            pl.BlockSpec((1, chunk, 1), lambda b, h, c: (h, 0, 0)),
            pl.BlockSpec((1, 1, 1), lambda b, h, c: (h, 0, 0)),
        ],
        out_specs=[
            pl.BlockSpec((1, chunk, RET_DV), lambda b, h, c: (b, c, h)),
            pl.BlockSpec((1, 1, RET_DK, RET_DV), lambda b, h, c: (b, h, 0, 0)),
        ],
        out_shape=[jax.ShapeDtypeStruct((batch, seq, RET_V), BF16),
                   jax.ShapeDtypeStruct((batch, RET_HEADS, RET_DK, RET_DV), F32)],
        scratch_shapes=[pltpu.VMEM((RET_DK, RET_DV), F32)],
        compiler_params=_cparams(("parallel", "parallel", "arbitrary")),
        name="retention")(qk3, qk3, vg3, vg3, r0, dm, qd, kd, gl)
    return y.reshape(batch * seq, RET_V), r_fin


def _diff_lambda(lam_ref, layer_idx):
    lv = lam_ref[...]
    lam_init = 0.8 - 0.6 * math.exp(-0.3 * layer_idx)
    t1 = jnp.sum(lv[0:1, :] * lv[1:2, :], axis=-1, keepdims=True)
    t2 = jnp.sum(lv[2:3, :] * lv[3:4, :], axis=-1, keepdims=True)
    return jnp.exp(t1) - jnp.exp(t2) + lam_init, lam_init


def _attn_kernel(qi_ref, kj_ref, q_ref, k_ref, vt_ref, lam_ref, sg_ref, o_ref,
                 qz_scr, m_scr, l_scr, acc_scr, *, tq, tk, kw, layer_idx):
    p = pl.program_id(2)
    qi = qi_ref[p]
    kj = kj_ref[p]
    last_kj = (qi * tq + tq - 1) // tk

    @pl.when(kj == 0)
    def _():
        qt = jnp.transpose(q_ref[0].astype(F32))
        feat = lax.broadcasted_iota(jnp.int32, qt.shape, 0)
        zero = jnp.zeros_like(qt)
        qz_scr[:, 0:tq] = jnp.where(feat < DA_DK, qt, zero).astype(BF16)
        qz_scr[:, tq:2 * tq] = jnp.where(feat >= DA_DK, qt, zero).astype(BF16)
        m_scr[...] = jnp.full(m_scr.shape, -jnp.inf, F32)
        l_scr[...] = jnp.zeros(l_scr.shape, F32)
        acc_scr[...] = jnp.zeros(acc_scr.shape, F32)

    def process(masked):
        m_run = m_scr[...]
        l_run = l_scr[...]
        acc = acc_scr[...]
        for sb in range(tk // kw):
            s = jnp.dot(k_ref[0, sb * kw:(sb + 1) * kw, :], qz_scr[...],
                        preferred_element_type=F32)
            if masked:
                row = lax.broadcasted_iota(jnp.int32, s.shape, 0)
                col = lax.broadcasted_iota(jnp.int32, s.shape, 1)
                kpos = kj * tk + sb * kw + row
                qpos = qi * tq + jnp.where(col >= tq, col - tq, col)
                s = jnp.where(kpos <= qpos, s, -jnp.inf)
            m_new = jnp.maximum(m_run, jnp.max(s, axis=0, keepdims=True))
            alpha = jnp.exp(m_run - m_new)
            pm = jnp.exp(s - m_new)
            l_run = alpha * l_run + jnp.sum(pm, axis=0, keepdims=True)
            acc = alpha * acc + jnp.dot(vt_ref[:, sb * kw:(sb + 1) * kw], pm.astype(BF16),
                                        preferred_element_type=F32)
            m_run = m_new
        m_scr[...] = m_run
        l_scr[...] = l_run
        acc_scr[...] = acc

    @pl.when(kj < last_kj)
    def _():
        process(False)

    @pl.when(kj == last_kj)
    def _():
        process(True)
        lam, lam_init = _diff_lambda(lam_ref, layer_idx)
        o = acc_scr[...] / l_scr[...]
        od = o[:, 0:tq] - lam * o[:, tq:2 * tq]
        od = od * lax.rsqrt(jnp.mean(od * od, axis=0, keepdims=True) + NORM_EPS)
        o_ref[...] = (od * sg_ref[...] * (1.0 - lam_init)).astype(o_ref.dtype)


def _attn_prompt(qb, kb, vt, lam_vec, subln_g, *, batch, seq, layer_idx, tq=512, tk=1024, kw=512):
    tq = min(tq, seq)
    tk = min(tk, seq)
    kw = min(kw, tk)
    nq, nk = seq // tq, seq // tk
    pairs = [(i, j) for i in range(nq) for j in range((i * tq + tq - 1) // tk + 1)]
    qi = jnp.asarray([pq for pq, _ in pairs], jnp.int32)
    kj = jnp.asarray([pk for _, pk in pairs], jnp.int32)
    q3 = qb.reshape(batch, seq, DA_QK)
    k3 = kb.reshape(batch, seq, DA_QK)
    kern = functools.partial(_attn_kernel, tq=tq, tk=tk, kw=kw, layer_idx=layer_idx)
    grid_spec = pltpu.PrefetchScalarGridSpec(
        num_scalar_prefetch=2, grid=(batch, DA_HEADS, len(pairs)),
        in_specs=[
            pl.BlockSpec((1, tq, LANES), lambda b, h, p, qi, kj: (b, qi[p], h)),
            pl.BlockSpec((1, tk, LANES), lambda b, h, p, qi, kj: (b, kj[p], h)),
            pl.BlockSpec((DA_DV, tk), lambda b, h, p, qi, kj: (h, b * nk + kj[p])),
            pl.BlockSpec((4, DA_DK), lambda b, h, p, qi, kj: (0, 0)),
            pl.BlockSpec((DA_DV, 1), lambda b, h, p, qi, kj: (0, 0)),
        ],
        out_specs=pl.BlockSpec((DA_DV, tq), lambda b, h, p, qi, kj: (h, b * nq + qi[p])),
        scratch_shapes=[pltpu.VMEM((LANES, 2 * tq), BF16),
                        pltpu.VMEM((1, 2 * tq), F32),
                        pltpu.VMEM((1, 2 * tq), F32),
                        pltpu.VMEM((DA_DV, 2 * tq), F32)])
    return pl.pallas_call(
        kern, grid_spec=grid_spec,
        out_shape=jax.ShapeDtypeStruct((DA_V, batch * seq), BF16),
        compiler_params=_cparams(("parallel", "parallel", "arbitrary")),
        name="attn_prompt")(qi, kj, q3, k3, vt, lam_vec, subln_g.reshape(DA_DV, 1))


def _attn_sample_kernel(pt_ref, q_ref, *refs, pp, n_steps, n_new, layer_idx):
    k_refs = refs[:pp]
    v_refs = refs[pp:2 * pp]
    kn_ref, vn_ref, lam_ref, sg_ref, o_ref, m_scr, l_scr, acc_scr = refs[2 * pp:]
    step = pl.program_id(1)
    rows = q_ref.shape[1]

    @pl.when(step == 0)
    def _():
        m_scr[...] = jnp.full(m_scr.shape, -jnp.inf, F32)
        l_scr[...] = jnp.zeros(l_scr.shape, F32)
        acc_scr[...] = jnp.zeros(acc_scr.shape, F32)

    row = lax.broadcasted_iota(jnp.int32, (rows, LANES), 0)
    lane = lax.broadcasted_iota(jnp.int32, (rows, LANES), 1)
    keep = ((row < n_new) == (lane < DA_DK)) & (row < 2 * n_new)

    def head_q(h):
        qh = q_ref[0][:, h * LANES:(h + 1) * LANES]
        return jnp.where(keep, qh, jnp.zeros_like(qh))

    for h in range(DA_HEADS):
        qz = head_q(h)
        cs = slice(h * LANES, (h + 1) * LANES)
        s_list = [lax.dot_general(qz, k_refs[u][0][:, cs].astype(BF16), NT_DIMS,
                                  preferred_element_type=F32) for u in range(pp)]
        mx = s_list[0].max(axis=-1, keepdims=True)
        for u in range(1, pp):
            mx = jnp.maximum(mx, s_list[u].max(axis=-1, keepdims=True))
        m_old = m_scr[h][:, 0:1]
        m_new = jnp.maximum(m_old, mx)
        alpha = jnp.exp(m_old - m_new)
        l_new = alpha * l_scr[h][:, 0:1]
        acc = alpha * acc_scr[h]
        for u in range(pp):
            pu = jnp.exp(s_list[u] - m_new)
            l_new = l_new + jnp.sum(pu, axis=-1, keepdims=True)
            acc = acc + jnp.dot(pu.astype(BF16), v_refs[u][0, :, h, :].astype(BF16),
                                preferred_element_type=F32)
        m_scr[h] = jnp.broadcast_to(m_new, (rows, LANES))
        l_scr[h] = jnp.broadcast_to(l_new, (rows, LANES))
        acc_scr[h] = acc

    @pl.when(step == n_steps - 1)
    def _():
        lam, lam_init = _diff_lambda(lam_ref, layer_idx)
        tok = jnp.where(row >= n_new, row - n_new, row)[:, 0:1]
        for h in range(DA_HEADS):
            qf = head_q(h).astype(F32)
            cs = slice(h * LANES, (h + 1) * LANES)
            kn = kn_ref[0][:, cs]
            vn = vn_ref[0][:, cs]
            s_new = []
            for j in range(n_new):
                sj = jnp.sum(qf * kn[j:j + 1, :], axis=-1, keepdims=True)
                s_new.append(jnp.where(tok >= j, sj, -jnp.inf))
            m_old = m_scr[h][:, 0:1]
            m_new = m_old
            for sj in s_new:
                m_new = jnp.maximum(m_new, sj)
            alpha = jnp.exp(m_old - m_new)
            l_new = alpha * l_scr[h][:, 0:1]
            acc = alpha * acc_scr[h]
            for j in range(n_new):
                pj = jnp.exp(s_new[j] - m_new)
                l_new = l_new + pj
                acc = acc + pj * vn[j:j + 1, :]
            o = acc / l_new
            od = o[0:n_new, :] - lam * o[n_new:2 * n_new, :]
            od = od * lax.rsqrt(jnp.mean(od * od, axis=-1, keepdims=True) + NORM_EPS)
            o_ref[0, :, cs] = od * sg_ref[...] * (1.0 - lam_init)


def _attn_sample(qb, k_new, v_new, cache_k, cache_v, page_table, lam_vec, subln_g, *,
                 batch, n_new, layer_idx, pp=4):
    n_pages = page_table.shape[1]
    n_phys = cache_k.shape[0]
    n_steps = n_pages // pp
    q3 = qb.reshape(batch, n_new, DA_QK)
    rows = 16
    q3 = jnp.concatenate([q3, q3, jnp.zeros((batch, rows - 2 * n_new, DA_QK), q3.dtype)], axis=1)
    kc = cache_k.reshape(n_phys, PAGE_SIZE, DA_QK)
    vc = cache_v
    kn = k_new.reshape(batch, n_new, DA_QK)
    vn = v_new.reshape(batch, n_new, DA_V)

    def page_spec(u):
        return pl.BlockSpec((1, PAGE_SIZE, DA_QK), lambda b, s, pt, u=u: (pt[b, s * pp + u], 0, 0))

    def vpage_spec(u):
        return pl.BlockSpec((1, PAGE_SIZE, DA_HEADS, DA_DV),
                            lambda b, s, pt, u=u: (pt[b, s * pp + u], 0, 0, 0))

    in_specs = ([pl.BlockSpec((1, rows, DA_QK), lambda b, s, pt: (b, 0, 0))]
                + [page_spec(u) for u in range(pp)] + [vpage_spec(u) for u in range(pp)]
                + [pl.BlockSpec((1, n_new, DA_QK), lambda b, s, pt: (b, 0, 0)),
                   pl.BlockSpec((1, n_new, DA_V), lambda b, s, pt: (b, 0, 0)),
                   pl.BlockSpec((4, DA_DK), lambda b, s, pt: (0, 0)),
                   pl.BlockSpec((1, DA_DV), lambda b, s, pt: (0, 0))])
    grid_spec = pltpu.PrefetchScalarGridSpec(
        num_scalar_prefetch=1, grid=(batch, n_steps), in_specs=in_specs,
        out_specs=pl.BlockSpec((1, n_new, DA_V), lambda b, s, pt: (b, 0, 0)),
        scratch_shapes=[pltpu.VMEM((DA_HEADS, rows, LANES), F32)] * 3)
    kern = functools.partial(_attn_sample_kernel, pp=pp, n_steps=n_steps, n_new=n_new,
                             layer_idx=layer_idx)
    y = pl.pallas_call(
        kern, grid_spec=grid_spec,
        out_shape=jax.ShapeDtypeStruct((batch, n_new, DA_V), F32),
        compiler_params=_cparams(("parallel", "arbitrary")),
        name="attn_sample")(page_table, q3, *([kc] * pp), *([vc] * pp), kn, vn, lam_vec,
                            subln_g.reshape(1, DA_DV))
    return y.reshape(batch * n_new, DA_V)


def _peer_scores_kernel(x_ref, wq_ref, sk_ref, s1_ref, s2_ref):
    q = jnp.dot(x_ref[...].astype(BF16), wq_ref[...], preferred_element_type=F32)
    for h in range(PEER_HEADS):
        for c in range(2):
            col = (h * 2 + c) * LANES
            st = lax.dot_general(sk_ref[c], q[:, col:col + LANES], NT_DIMS,
                                 precision=lax.Precision.HIGHEST, preferred_element_type=F32)
            (s1_ref if c == 0 else s2_ref)[h] = st


def _peer_scores(x, wq, sk, *, tm=512):
    t = x.shape[0]
    tm = min(tm, t)
    shp = jax.ShapeDtypeStruct((PEER_HEADS, PEER_NKEYS, t), F32)
    spec = pl.BlockSpec((PEER_HEADS, PEER_NKEYS, tm), lambda i: (0, 0, i))
    return pl.pallas_call(
        _peer_scores_kernel, grid=(t // tm,),
        in_specs=[pl.BlockSpec((tm, D_MODEL), lambda i: (i, 0)),
                  pl.BlockSpec((D_MODEL, PEER_HEADS * PEER_DKEY), lambda i: (0, 0)),
                  pl.BlockSpec((2, PEER_NKEYS, PEER_DKEY // 2), lambda i: (0, 0, 0))],
        out_specs=[spec, spec], out_shape=[shp, shp],
        compiler_params=_cparams(("parallel",)),
        name="peer_scores")(x, wq, sk)


def _peer_topk_kernel(s1_ref, s2_ref, thr_ref, a1_ref, key2_ref, a2_ref, rk1_scr):
    shp = s1_ref.shape[2:]
    neg = jnp.full(shp, -jnp.inf, F32)
    k = PEER_TOPK
    unranked = F32(k)

    def rank(s_ref, rk_ref):
        def init(n, carry):
            rk_ref[0, n] = jnp.full(shp, unranked, F32)
            return carry
        lax.fori_loop(0, PEER_NKEYS, init, 0)
        vals = []
        none = jnp.full(shp, -1.0, F32)
        win = none
        nch = 8
        for it in range(k + 1):
            def body(i, carry, it=it, win=win):
                ms, mis = carry
                new_ms, new_mis = [], []
                for p in range(nch):
                    n = i * nch + p
                    nf = F32(n)
                    r = rk_ref[0, n]
                    if it > 0:
                        r = jnp.where(win == nf, F32(it - 1), r)
                        rk_ref[0, n] = r
                    if it < k:
                        w = s_ref[0, n]
                        better = (r > k - 0.5) & (w > ms[p])
                        new_ms.append(jnp.where(better, w, ms[p]))
                        new_mis.append(jnp.where(better, nf, mis[p]))
                return (tuple(new_ms), tuple(new_mis)) if it < k else carry
            carry = ((neg,) * nch, (none,) * nch)
            for i in range(PEER_NKEYS // nch):
                carry = body(i, carry)
            ms, mis = carry
            if it < k:
                m, mi = ms[0], mis[0]
                for p in range(1, nch):
                    better = (ms[p] > m) | ((ms[p] == m) & (mis[p] < mi))
                    m = jnp.where(better, ms[p], m)
                    mi = jnp.where(better, mis[p], mi)
                vals.append(m)
                win = mi
        return vals

    a = rank(s1_ref, rk1_scr)
    b = rank(s2_ref, key2_ref)

    pairs = [(i, j) for i in range(k) for j in range(k) if (i + 1) * (j + 1) <= k]
    cv = [a[i] + b[j] for i, j in pairs]
    sel = [jnp.zeros(shp, F32) for _ in pairs]
    c0 = None
    z = jnp.zeros(shp, F32)
    for it in range(k):
        m = cv[0]
        for c in cv[1:]:
            m = jnp.maximum(m, c)
        if it == 0:
            c0 = m
        z = z + jnp.exp(m - c0)
        found = jnp.zeros(shp, F32)
        for idx in range(len(pairs)):
            hit = (cv[idx] == m) & (found < 0.5)
            sel[idx] = jnp.where(hit, 1.0, sel[idx])
            cv[idx] = jnp.where(hit, neg, cv[idx])
            found = jnp.where(hit, 1.0, found)
    cnt = [jnp.zeros(shp, F32) for _ in range(k)]
    for idx, (i, j) in enumerate(pairs):
        cnt[i] = cnt[i] + sel[idx]
    inv_z = 1.0 / z

    def emit(n, carry):
        r1 = rk1_scr[0, n]
        taken = jnp.zeros(shp, F32)
        for i in range(k):
            taken = jnp.where(r1 == F32(i), cnt[i], taken)
        thr_ref[0, n] = taken - 0.5
        a1_ref[0, n] = jnp.exp(s1_ref[0, n] - a[0]) * inv_z
        a2_ref[0, n] = jnp.exp(s2_ref[0, n] - b[0])
        return carry
    lax.fori_loop(0, PEER_NKEYS, emit, 0, unroll=4)


def _peer_topk(s1, s2):
    t = s1.shape[2]
    nt = t // LANES
    ts = min(8, nt)
    s1v = s1.reshape(PEER_HEADS, PEER_NKEYS, nt, LANES)
    s2v = s2.reshape(PEER_HEADS, PEER_NKEYS, nt, LANES)
    spec = pl.BlockSpec((1, PEER_NKEYS, ts, LANES), lambda h, i: (h, 0, i, 0))
    shp = jax.ShapeDtypeStruct(s1v.shape, F32)
    thr, a1, key2, a2 = pl.pallas_call(
        _peer_topk_kernel, grid=(PEER_HEADS, nt // ts),
        in_specs=[spec, spec], out_specs=[spec] * 4, out_shape=[shp] * 4,
        scratch_shapes=[pltpu.VMEM((1, PEER_NKEYS, ts, LANES), F32)],
        compiler_params=_cparams(("parallel", "parallel")),
        name="peer_topk")(s1v, s2v)
    flat = (PEER_HEADS, PEER_NKEYS, t)
    return thr.reshape(flat), a1.reshape(flat), key2.reshape(flat), a2.reshape(flat)


def _peer_ffn_kernel(x_ref, u_ref, vt_ref, thr_ref, a1_ref, key2_ref, a2_ref, g_ref, b_ref,
                     o_ref, xb_scr, acc_scr, *sub_scr, n_steps, nsub, gsub, cw):
    e = pl.program_id(1)
    tt = x_ref.shape[0]
    h_scr, a_scrs = sub_scr[0], sub_scr[1:]
    sub = gsub * PEER_NKEYS
    pk = 16
    grp = PEER_NKEYS // pk

    @pl.when(e == 0)
    def _():
        xb_scr[...] = jnp.transpose(x_ref[...]).astype(BF16)
        acc_scr[...] = jnp.zeros(acc_scr.shape, F32)

    h_scr[...] = jnp.dot(u_ref[...], xb_scr[...], preferred_element_type=F32)
    for s in range(nsub):
        for gl in range(gsub):
            gi = s * gsub + gl
            rs = slice(gl * PEER_NKEYS, (gl + 1) * PEER_NKEYS)
            for cc in range(tt // cw):
                cs = slice(cc * cw, (cc + 1) * cw)
                wd = jnp.zeros((grp, pk, cw), BF16)
                for h in range(PEER_HEADS):
                    thr_b = jnp.broadcast_to(thr_ref[0, 0, h, gi:gi + 1, cs], (pk, cw)).astype(BF16)
                    a1_b = jnp.broadcast_to(a1_ref[0, 0, h, gi:gi + 1, cs], (pk, cw)).astype(BF16)
                    k2 = key2_ref[h, :, cs].reshape(grp, pk, cw)
                    a2 = a2_ref[h, :, cs].reshape(grp, pk, cw)
                    wd = wd + jnp.where(k2 < thr_b[None], a2, jnp.zeros_like(a2)) * a1_b[None]
                hv = h_scr[gi * PEER_NKEYS:(gi + 1) * PEER_NKEYS, cs]
                gelu = 0.5 * hv * (1.0 + lax.erf(hv * F32(1.0 / math.sqrt(2.0))))
                a_scrs[s][rs, cs] = (gelu * wd.reshape(PEER_NKEYS, cw).astype(F32)).astype(BF16)
        acc_scr[...] += jnp.dot(vt_ref[0, :, s * sub:(s + 1) * sub], a_scrs[s][...],
                                preferred_element_type=F32)

    @pl.when(e == n_steps - 1)
    def _():
        y = jnp.transpose(acc_scr[...])
        o_ref[...] = _layer_norm(ALPHA * x_ref[...] + y, g_ref[...], b_ref[...])


def _peer_ffn(x, u_bf, vt_bf, thr, a1, key2, a2, g, b, *, tt=512, cw=128, gsub=2):
    t = x.shape[0]
    key2 = key2.astype(BF16)
    a2 = a2.astype(BF16)
    te = vt_bf.shape[2]
    tt = min(tt, t)
    cw = min(cw, tt)
    ng = te // PEER_NKEYS
    nsub = ng // gsub
    n_steps = PEER_N // te

    def per_step(a):
        a = a.reshape(PEER_HEADS, n_steps, ng, t // tt, tt)
        return a.transpose(3, 1, 0, 2, 4)

    step_spec = pl.BlockSpec((1, 1, PEER_HEADS, ng, tt), lambda i, e: (i, e, 0, 0, 0))
    kern = functools.partial(_peer_ffn_kernel, n_steps=n_steps, nsub=nsub, gsub=gsub, cw=cw)
    return pl.pallas_call(
        kern, grid=(t // tt, n_steps),
        in_specs=[
            pl.BlockSpec((tt, D_MODEL), lambda i, e: (i, 0)),
            pl.BlockSpec((te, D_MODEL), lambda i, e: (e, 0)),
            pl.BlockSpec((1, D_MODEL, te), lambda i, e: (e, 0, 0)),
            step_spec,
            step_spec,
            pl.BlockSpec((PEER_HEADS, PEER_NKEYS, tt), lambda i, e: (0, 0, i)),
            pl.BlockSpec((PEER_HEADS, PEER_NKEYS, tt), lambda i, e: (0, 0, i)),
            pl.BlockSpec((1, D_MODEL), lambda i, e: (0, 0)),
            pl.BlockSpec((1, D_MODEL), lambda i, e: (0, 0)),
        ],
        out_specs=pl.BlockSpec((tt, D_MODEL), lambda i, e: (i, 0)),
        out_shape=jax.ShapeDtypeStruct((t, D_MODEL), F32),
        scratch_shapes=([pltpu.VMEM((D_MODEL, tt), BF16), pltpu.VMEM((D_MODEL, tt), F32)]
                        + [pltpu.VMEM((te, tt), F32)]
                        + [pltpu.VMEM((gsub * PEER_NKEYS, tt), BF16)] * nsub),
        compiler_params=_cparams(("parallel", "arbitrary")),
        name="peer_ffn")(x, u_bf, vt_bf, per_step(thr), per_step(a1), key2, a2,
                         g.reshape(1, D_MODEL), b.reshape(1, D_MODEL))


def _peer_layer(x, wq_bf, sk, u_bf, vt_bf, g, b):
    s1, s2 = _peer_scores(x, wq_bf, sk)
    thr, a1, key2, a2 = _peer_topk(s1, s2)
    return _peer_ffn(x, u_bf, vt_bf, thr, a1, key2, a2, g, b)


def _ret_rot_tables(pos):
    half = RET_DK // 2
    inv = 1.0 / (RET_THETA ** (jnp.arange(half, dtype=F32) / half))
    ang = pos.astype(F32)[:, None] * inv[None, :]
    return jnp.cos(ang), jnp.sin(ang)


def _da_rot_tables(pos):
    half = DA_ROT // 2
    inv = 1.0 / (ROPE_THETA ** (jnp.arange(half, dtype=F32) / half))
    ang = pos.astype(F32)[:, None] * inv[None, :]
    cos, sin = jnp.cos(ang), jnp.sin(ang)
    n = pos.shape[0]
    pad = jnp.zeros((n, DA_DK - DA_ROT), F32)
    c64 = jnp.concatenate([cos, cos, pad + 1.0], axis=1)
    lo64 = jnp.concatenate([-sin, jnp.zeros((n, half), F32), pad], axis=1)
    hi64 = jnp.concatenate([jnp.zeros((n, half), F32), sin, pad], axis=1)
    rep = LANES // DA_DK
    return jnp.tile(c64, (1, rep)), jnp.tile(lo64, (1, rep)), jnp.tile(hi64, (1, rep))


def _ret_layer(x, r0, w_in_bf, w_out_bf, rot_tabs, g, b, *, batch, seq, true_len, tm):
    qk = _proj(x, w_in_bf[:, :2 * RET_QK], tm=tm, tn=RET_DK, out_dtypes=(F32,), rot="full",
               tabs=rot_tabs, scales=(1.0, RET_DK ** -0.5), n_scale_split=RET_HEADS)[0]
    vg = _proj(x, w_in_bf[:, 2 * RET_QK:], tm=tm, tn=512, out_dtypes=(F32,))[0]
    if seq != true_len:
        pad = ((0, 0), (0, seq - true_len), (0, 0))
        qk = jnp.pad(qk.reshape(batch, true_len, -1), pad).reshape(batch * seq, -1)
        vg = jnp.pad(vg.reshape(batch, true_len, -1), pad).reshape(batch * seq, -1)
    y, r_fin = _retention(qk, vg, r0, batch=batch, seq=seq, chunk=RET_CHUNK, true_len=min(true_len, RET_CHUNK))
    if seq != true_len:
        y = y.reshape(batch, seq, RET_V)[:, :true_len].reshape(batch * true_len, RET_V)
    return _out_ln(y, w_out_bf, x, g, b, tm=tm), r_fin


def _da_project(x, w_in_bf, rot_tabs, *, tm):
    q = _proj(x, w_in_bf[:, :DA_QK], tm=tm, tn=512, out_dtypes=(BF16,), rot="partial",
              tabs=rot_tabs, scales=(DA_DK ** -0.5, DA_DK ** -0.5), n_scale_split=0)[0]
    k, kb = _proj(x, w_in_bf[:, DA_QK:2 * DA_QK], tm=tm, tn=512, out_dtypes=(F32, BF16),
                  rot="partial", tabs=rot_tabs)
    v = _proj(x, w_in_bf[:, 2 * DA_QK:], tm=tm, tn=512, out_dtypes=(F32,))[0]
    return q, k, kb, v


def kernel(x_prompt, x_sample, state_ret_l0, cache_k_l1, cache_v_l1, state_ret_l2, cache_k_l3,
           cache_v_l3, page_table, ret_w_in, ret_w_out, da_w_in, da_w_out, da_lambda, da_subln_g,
           peer_w_q, peer_sub_keys, peer_u, peer_v, ln_g, ln_b):
    bp, sp, _ = x_prompt.shape
    bs, ts, _ = x_sample.shape
    xp = x_prompt.reshape(bp * sp, D_MODEL)
    xs = x_sample.reshape(bs * ts, D_MODEL)
    ret_states = [state_ret_l0, state_ret_l2]
    da_caches = [(cache_k_l1, cache_v_l1), (cache_k_l3, cache_v_l3)]

    pos_p = jnp.arange(sp)
    pos_s = jnp.tile(PAST_LEN + jnp.arange(ts), bs)
    ret_tab_p, ret_tab_s = _ret_rot_tables(pos_p), _ret_rot_tables(pos_s)
    da_tab_p, da_tab_s = _da_rot_tables(pos_p), _da_rot_tables(pos_s)
    tm_p, tm_s = 512, bs * ts
    te = 1024

    new_states = []
    for i in range(DEPTH):
        j = i // 2
        if i % 2 == 0:
            w_in = ret_w_in[j].astype(BF16)
            w_out = ret_w_out[j].astype(BF16)
            r0p = jnp.zeros((bp, RET_HEADS, RET_DK, RET_DV), F32)
            xp, rp = _ret_layer(xp, r0p, w_in, w_out, ret_tab_p, ln_g[i, 0], ln_b[i, 0],
                                batch=bp, seq=sp, true_len=sp, tm=tm_p)
            xs, rs = _ret_layer(xs, ret_states[j], w_in, w_out, ret_tab_s, ln_g[i, 0], ln_b[i, 0],
                                batch=bs, seq=RET_CHUNK, true_len=ts, tm=tm_s)
            new_states.append((rp, rs))
        else:
            w_in = da_w_in[j].astype(BF16)
            w_out = da_w_out[j].astype(BF16)
            ck, cv = da_caches[j]
            qp, kp, kpb, vp = _da_project(xp, w_in, da_tab_p, tm=tm_p)
            vtp = _proj_t(xp, w_in[:, 2 * DA_QK:].T, tm=tm_p, out_dtype=BF16)
            yp = _attn_prompt(qp, kpb, vtp, da_lambda[j], da_subln_g[j], batch=bp, seq=sp,
                              layer_idx=i)
            qs, ks, _, vs = _da_project(xs, w_in, da_tab_s, tm=tm_s)
            ys = _attn_sample(qs, ks, vs, ck, cv, page_table, da_lambda[j], da_subln_g[j],
                              batch=bs, n_new=ts, layer_idx=i)
            xp = _out_ln(yp, w_out, xp, ln_g[i, 0], ln_b[i, 0], tm=tm_p, y_transposed=True)
            xs = _out_ln(ys, w_out, xs, ln_g[i, 0], ln_b[i, 0], tm=tm_s)
            new_states.append((kp.reshape(bp, sp, DA_HEADS, 2, DA_DK),
                               vp.reshape(bp, sp, DA_HEADS, DA_DV),
                               ks.reshape(bs, ts, DA_HEADS, 2, DA_DK),
                               vs.reshape(bs, ts, DA_HEADS, DA_DV)))
        wq = peer_w_q[i].astype(BF16)
        u_bf = peer_u[i].astype(BF16)
        vt_bf = peer_v[i].astype(BF16).reshape(PEER_N // te, te, D_MODEL).transpose(0, 2, 1)
        sk = peer_sub_keys[i]
        xp = _peer_layer(xp, wq, sk, u_bf, vt_bf, ln_g[i, 1], ln_b[i, 1])
        xs = _peer_layer(xs, wq, sk, u_bf, vt_bf, ln_g[i, 1], ln_b[i, 1])

    (rp0, rs0), (kp1, vp1, ks1, vs1), (rp2, rs2), (kp3, vp3, ks3, vs3) = new_states
    return (xp.reshape(bp, sp, D_MODEL), xs.reshape(bs, ts, D_MODEL), rp0, rs0, kp1, vp1, ks1, vs1,
            rp2, rs2, kp3, vp3, ks3, vs3)
```
